```python
import math
import jax
import jax.numpy as jnp
from jax import lax
import numpy as np

D_MODEL = 4096
BATCH = 1
SEQ = 16384
DEPTH = 4
DEC_BATCH = 16
DEC_SEQ = 64
PAST_LEN = 1024

CHUNK = 64
N_MIXERS = 3
D_FF = 11008
FFN_RES = 0.5
EPS = 1e-6
NEG_INF = -1e30

H_A = 32
HD_A = D_MODEL // H_A
NB_A = 8
REL_CLIP = 128

H_B = 32
HD_B = D_MODEL // H_B
QBLOCK_B = 128
F_BIAS_INIT = 3.0

H_C = 64
HD_C = 64
KV_C = 8
G_C = H_C // KV_C
WINDOW_C = 128
NB_C = WINDOW_C // CHUNK
T5_BUCKETS = 32
T5_MAX_DIST = 128

N_LAYERS_A = (DEPTH + 2) // 3
N_LAYERS_B = (DEPTH + 1) // 3
N_LAYERS_C = DEPTH // 3

kernel_name = 'hybrid_streaming_encoder_step'


def rms_norm(x, g):
    xf = x.astype(jnp.float32)
    y = xf * lax.rsqrt(jnp.mean(xf * xf, axis=-1, keepdims=True) + EPS)
    return (y * g.astype(jnp.float32)).astype(x.dtype)


def swiglu(x, w_in, w_out):
    gate, up = jnp.split(x @ w_in, 2, axis=-1)
    return (jax.nn.silu(gate) * up) @ w_out


def attend(q, k, v, bias, valid, sink=None):
    scale = q.shape[-1] ** -0.5
    s = jnp.einsum('bqhgd,bkhd->bhgqk', q, k, preferred_element_type=jnp.float32) * scale + bias
    s = jnp.where(valid, s, NEG_INF)
    if sink is None:
        p = jax.nn.softmax(s, axis=-1)
    else:
        sk = sink.astype(jnp.float32)[None, :, :, None, None]
        m = jnp.maximum(jnp.max(s, axis=-1, keepdims=True), sk)
        e = jnp.exp(s - m)
        p = e / (jnp.sum(e, axis=-1, keepdims=True) + jnp.exp(sk - m))
    return jnp.einsum('bhgqk,bkhd->bqhgd', p.astype(v.dtype), v)


def band_valid(q_pos, k_pos, n_prev):
    qc = q_pos[:, None] // CHUNK
    kc = k_pos[None, :] // CHUNK
    return (k_pos[None, :] >= 0) & (kc <= qc) & (kc >= qc - n_prev)


def band_attention_prompt(q, k, v, n_prev, bias_fn, sink):
    b, s = q.shape[:2]
    n_chunks = s // CHUNK
    band = (n_prev + 1) * CHUNK
    pad = [(0, 0), (n_prev * CHUNK, 0), (0, 0), (0, 0)]
    kp = jnp.pad(k, pad)
    vp = jnp.pad(v, pad)
    qc = q.reshape(b, n_chunks, CHUNK, *q.shape[2:]).swapaxes(0, 1)

    def one_chunk(args):
        c, qb = args
        start = c * CHUNK
        kb = lax.dynamic_slice_in_dim(kp, start, band, axis=1)
        vb = lax.dynamic_slice_in_dim(vp, start, band, axis=1)
        q_pos = start + jnp.arange(CHUNK)
        k_pos = start - n_prev * CHUNK + jnp.arange(band)
        return attend(qb, kb, vb, bias_fn(q_pos, k_pos), band_valid(q_pos, k_pos, n_prev), sink)

    out = lax.map(one_chunk, (jnp.arange(n_chunks), qc))
    return out.swapaxes(0, 1).reshape(q.shape)


def band_attention_sample(q, k, v, cache_k, cache_v, n_prev, bias_fn, sink):
    t, w = q.shape[1], cache_k.shape[1]
    kk = jnp.concatenate([cache_k.astype(k.dtype), k], axis=1)
    vv = jnp.concatenate([cache_v.astype(v.dtype), v], axis=1)
    q_pos = PAST_LEN + jnp.arange(t)
    k_pos = PAST_LEN - w + jnp.arange(w + t)
    o = attend(q, kk, vv, bias_fn(q_pos, k_pos), band_valid(q_pos, k_pos, n_prev), sink)
    return o, kk[:, t:], vv[:, t:]


def t5_bucket(rel):
    half = T5_BUCKETS // 2
    max_exact = half // 2
    n = jnp.abs(rel)
    ratio = jnp.maximum(n, max_exact).astype(jnp.float32) / max_exact
    large = max_exact + (jnp.log(ratio) / math.log(T5_MAX_DIST / max_exact) * (half - max_exact)).astype(jnp.int32)
    large = jnp.minimum(large, half - 1)
    return jnp.where(rel < 0, half, 0) + jnp.where(n < max_exact, n, large)


def mixer_a(h, cache_k, cache_v, w_qkv, w_o, qk_g, rel_bias):
    b, t, _ = h.shape
    q, k, v = jnp.split(h @ w_qkv, 3, axis=-1)
    q = rms_norm(q.reshape(b, t, H_A, HD_A), qk_g[0])[:, :, :, None, :]
    k = rms_norm(k.reshape(b, t, H_A, HD_A), qk_g[1])
    v = v.reshape(b, t, H_A, HD_A)
    table = rel_bias.astype(jnp.float32)

    def bias_fn(q_pos, k_pos):
        rel = jnp.clip(q_pos[:, None] - k_pos[None, :], -REL_CLIP, REL_CLIP) + REL_CLIP
        return table[rel].transpose(2, 0, 1)[None, :, None]

    if cache_k is None:
        o = band_attention_prompt(q, k, v, NB_A, bias_fn, None)
        w = min(NB_A * CHUNK, t)
        new_k, new_v = k[:, t - w:], v[:, t - w:]
    else:
        o, new_k, new_v = band_attention_sample(q, k, v, cache_k, cache_v, NB_A, bias_fn, None)
    return o.reshape(b, t, H_A * HD_A) @ w_o, (new_k, new_v)


def fox_prompt(q, k, v, cum):
    b, s = q.shape[:2]
    nb = s // QBLOCK_B
    qb = q.reshape(b, nb, QBLOCK_B, H_B, 1, HD_B).swapaxes(0, 1)
    cb = cum.reshape(b, nb, QBLOCK_B, H_B).swapaxes(0, 1)
    cum_k = cum.transpose(0, 2, 1)
    k_pos = jnp.arange(s)

    def one_block(args):
        i, qi, ci = args
        q_pos = i * QBLOCK_B + jnp.arange(QBLOCK_B)
        bias = ci.transpose(0, 2, 1)[:, :, :, None] - cum_k[:, :, None, :]
        valid = k_pos[None, :] <= q_pos[:, None]
        return attend(qi, k, v, bias[:, :, None], valid, None)

    out = lax.map(one_block, (jnp.arange(nb), qb, cb))
    return out.swapaxes(0, 1).reshape(q.shape)


def mixer_b(h, cache_k, cache_v, cache_logf, w_qkvf, f_bias, w_o, qk_g):
    b, t, _ = h.shape
    hw = H_B * HD_B
    proj = h @ w_qkvf
    q = rms_norm(proj[..., :hw].reshape(b, t, H_B, HD_B), qk_g[0])[:, :, :, None, :]
    k = rms_norm(proj[..., hw:2 * hw].reshape(b, t, H_B, HD_B), qk_g[1])
    v = proj[..., 2 * hw:3 * hw].reshape(b, t, H_B, HD_B)
    logf = jax.nn.log_sigmoid((proj[..., 3 * hw:] + f_bias).astype(jnp.float32))
    if cache_k is None:
        o = fox_prompt(q, k, v, jnp.cumsum(logf, axis=1))
    else:
        p = cache_k.shape[1]
        kk = jnp.concatenate([cache_k.astype(k.dtype), k], axis=1)
        vv = jnp.concatenate([cache_v.astype(v.dtype), v], axis=1)
        cum = jnp.cumsum(jnp.concatenate([cache_logf.astype(jnp.float32), logf], axis=1), axis=1)
        cum_t = cum.transpose(0, 2, 1)
        bias = cum_t[:, :, p:, None] - cum_t[:, :, None, :]
        valid = jnp.arange(p + t)[None, :] <= (p + jnp.arange(t))[:, None]
        o = attend(q, kk, vv, bias[:, :, None], valid, None)
    return o.reshape(b, t, hw) @ w_o, (k, v, logf)


def mixer_c(h, cache_k, cache_v, w_qkv, w_o, qk_g, sinks, t5_table):
    b, t, _ = h.shape
    qw, kw = H_C * HD_C, KV_C * HD_C
    proj = h @ w_qkv
    q = rms_norm(proj[..., :qw].reshape(b, t, KV_C, G_C, HD_C), qk_g[0])
    k = rms_norm(proj[..., qw:qw + kw].reshape(b, t, KV_C, HD_C), qk_g[1])
    v = proj[..., qw + kw:].reshape(b, t, KV_C, HD_C)
    table = t5_table.astype(jnp.float32)
    sink = sinks.reshape(KV_C, G_C)

    def bias_fn(q_pos, k_pos):
        bk = t5_bucket(q_pos[:, None] - k_pos[None, :])
        return table[bk].transpose(2, 0, 1).reshape(KV_C, G_C, *bk.shape)[None]

    if cache_k is None:
        o = band_attention_prompt(q, k, v, NB_C, bias_fn, sink)
        w = min(WINDOW_C, t)
        new_k, new_v = k[:, t - w:], v[:, t - w:]
    else:
        o, new_k, new_v = band_attention_sample(q, k, v, cache_k, cache_v, NB_C, bias_fn, sink)
    return o.reshape(b, t, qw) @ w_o, (new_k, new_v)


def setup_inputs(seed: int = 0) -> dict:
    key = jax.random.key(seed)
    ks = jax.random.split(key, 25)

    def nrm(k, shape, scale):
        return jax.random.normal(k, shape, jnp.float32) * scale

    w_a = min(NB_A * CHUNK, PAST_LEN)
    w_c = min(WINDOW_C, PAST_LEN)
    qkv_c = (H_C + 2 * KV_C) * HD_C
    return {
        'x_prompt': nrm(ks[0], (BATCH, SEQ, D_MODEL), 1.0),
        'x_sample': nrm(ks[1], (DEC_BATCH, DEC_SEQ, D_MODEL), 1.0),
        'cache_a_k': nrm(ks[2], (N_LAYERS_A, DEC_BATCH, w_a, H_A, HD_A), 1.0),
        'cache_a_v': nrm(ks[3], (N_LAYERS_A, DEC_BATCH, w_a, H_A, HD_A), 1.0),
        'cache_b_k': nrm(ks[4], (N_LAYERS_B, DEC_BATCH, PAST_LEN, H_B, HD_B), 1.0),
        'cache_b_v': nrm(ks[5], (N_LAYERS_B, DEC_BATCH, PAST_LEN, H_B, HD_B), 1.0),
        'cache_b_logf': jax.nn.log_sigmoid(F_BIAS_INIT + nrm(ks[6], (N_LAYERS_B, DEC_BATCH, PAST_LEN, H_B), 1.0)),
        'cache_c_k': nrm(ks[7], (N_LAYERS_C, DEC_BATCH, w_c, KV_C, HD_C), 1.0),
        'cache_c_v': nrm(ks[8], (N_LAYERS_C, DEC_BATCH, w_c, KV_C, HD_C), 1.0),
        'norm_g': 1.0 + nrm(ks[9], (DEPTH, 3, D_MODEL), 0.02),
        'ffn_w_in': nrm(ks[10], (DEPTH, 2, D_MODEL, 2 * D_FF), D_MODEL ** -0.5),
        'ffn_w_out': nrm(ks[11], (DEPTH, 2, D_FF, D_MODEL), D_FF ** -0.5),
        'a_w_qkv': nrm(ks[12], (N_LAYERS_A, D_MODEL, 3 * H_A * HD_A), D_MODEL ** -0.5),
        'a_w_o': nrm(ks[13], (N_LAYERS_A, H_A * HD_A, D_MODEL), (H_A * HD_A) ** -0.5),
        'a_qk_g': 1.0 + nrm(ks[14], (N_LAYERS_A, 2, HD_A), 0.02),
        'a_rel_bias': nrm(ks[15], (N_LAYERS_A, 2 * REL_CLIP + 1, H_A), 0.1),
        'b_w_qkvf': nrm(ks[16], (N_LAYERS_B, D_MODEL, 3 * H_B * HD_B + H_B), D_MODEL ** -0.5),
        'b_f_bias': F_BIAS_INIT + nrm(ks[17], (N_LAYERS_B, H_B), 0.1),
        'b_w_o': nrm(ks[18], (N_LAYERS_B, H_B * HD_B, D_MODEL), (H_B * HD_B) ** -0.5),
        'b_qk_g': 1.0 + nrm(ks[19], (N_LAYERS_B, 2, HD_B), 0.02),
        'c_w_qkv': nrm(ks[20], (N_LAYERS_C, D_MODEL, qkv_c), D_MODEL ** -0.5),
        'c_w_o': nrm(ks[21], (N_LAYERS_C, H_C * HD_C, D_MODEL), (H_C * HD_C) ** -0.5),
        'c_qk_g': 1.0 + nrm(ks[22], (N_LAYERS_C, 2, HD_C), 0.02),
        'c_sinks': nrm(ks[23], (N_LAYERS_C, H_C), 0.5),
        't5_bias': nrm(ks[24], (T5_BUCKETS, H_C), 0.1),
    }


def half_ffn(x, g, w_in, w_out):
    return x + FFN_RES * swiglu(rms_norm(x, g), w_in, w_out)


def reference(x_prompt, x_sample, cache_a_k, cache_a_v, cache_b_k, cache_b_v, cache_b_logf,
              cache_c_k, cache_c_v, norm_g, ffn_w_in, ffn_w_out, a_w_qkv, a_w_o, a_qk_g, a_rel_bias,
              b_w_qkvf, b_f_bias, b_w_o, b_qk_g, c_w_qkv, c_w_o, c_qk_g, c_sinks, t5_bias):
    hp, hs = x_prompt, x_sample
    a_kp, a_vp, a_ks, a_vs = [], [], [], []
    b_kp, b_vp, b_lp, b_ks, b_vs, b_ls = [], [], [], [], [], []
    c_kp, c_vp, c_ks, c_vs = [], [], [], []
    for i in range(DEPTH):
        kind, j = i % N_MIXERS, i // N_MIXERS
        hp = half_ffn(hp, norm_g[i, 0], ffn_w_in[i, 0], ffn_w_out[i, 0])
        hs = half_ffn(hs, norm_g[i, 0], ffn_w_in[i, 0], ffn_w_out[i, 0])
        up, us = rms_norm(hp, norm_g[i, 1]), rms_norm(hs, norm_g[i, 1])
        if kind == 0:
            mp, (kp_, vp_) = mixer_a(up, None, None, a_w_qkv[j], a_w_o[j], a_qk_g[j], a_rel_bias[j])
            ms, (ks_, vs_) = mixer_a(us, cache_a_k[j], cache_a_v[j], a_w_qkv[j], a_w_o[j], a_qk_g[j], a_rel_bias[j])
            a_kp.append(kp_); a_vp.append(vp_); a_ks.append(ks_); a_vs.append(vs_)
        elif kind == 1:
            mp, (kp_, vp_, lp_) = mixer_b(up, None, None, None, b_w_qkvf[j], b_f_bias[j], b_w_o[j], b_qk_g[j])
            ms, (ks_, vs_, ls_) = mixer_b(us, cache_b_k[j], cache_b_v[j], cache_b_logf[j], b_w_qkvf[j], b_f_bias[j], b_w_o[j], b_qk_g[j])
            b_kp.append(kp_); b_vp.append(vp_); b_lp.append(lp_)
            b_ks.append(ks_); b_vs.append(vs_); b_ls.append(ls_)
        else:
            mp, (kp_, vp_) = mixer_c(up, None, None, c_w_qkv[j], c_w_o[j], c_qk_g[j], c_sinks[j], t5_bias)
            ms, (ks_, vs_) = mixer_c(us, cache_c_k[j], cache_c_v[j], c_w_qkv[j], c_w_o[j], c_qk_g[j], c_sinks[j], t5_bias)
            c_kp.append(kp_); c_vp.append(vp_); c_ks.append(ks_); c_vs.append(vs_)
        hp = hp + mp
        hs = hs + ms
        hp = half_ffn(hp, norm_g[i, 2], ffn_w_in[i, 1], ffn_w_out[i, 1])
        hs = half_ffn(hs, norm_g[i, 2], ffn_w_in[i, 1], ffn_w_out[i, 1])
    return (hp, hs,
            jnp.stack(a_kp), jnp.stack(a_vp), jnp.stack(a_ks), jnp.stack(a_vs),
            jnp.stack(b_kp), jnp.stack(b_vp), jnp.stack(b_lp), jnp.stack(b_ks), jnp.stack(b_vs), jnp.stack(b_ls),
            jnp.stack(c_kp), jnp.stack(c_vp), jnp.stack(c_ks), jnp.stack(c_vs))
```

```python
import functools
import math

import jax
import jax.numpy as jnp
from jax import lax
from jax.experimental import pallas as pl
from jax.experimental.pallas import tpu as pltpu

CHUNK = 64
NB_A = 8
EPS = 1e-6
NEG_INF = -1e30
FFN_RES = 0.5
T5_MAX_DIST = 128

V7X_LANES = 128
V7X_VMEM_BYTES = 64 * 1024 * 1024
VMEM_HEADROOM_BYTES = 6 * 1024 * 1024

F32 = jnp.float32
BF16 = jnp.bfloat16


def _params(semantics, block_bytes):
    limit = min(2 * block_bytes + VMEM_HEADROOM_BYTES + 8 * 1024 * 1024, V7X_VMEM_BYTES - 2 * 1024 * 1024)
    return pltpu.CompilerParams(dimension_semantics=semantics, vmem_limit_bytes=int(limit))


def _row_tile(m, want):
    t = want
    while m % t:
        t //= 2
    return t


def _dot(a, b):
    return jnp.dot(a, b, preferred_element_type=F32)


def _dot_nt(a, b):
    return lax.dot_general(a, b, (((1,), (1,)), ((), ())), preferred_element_type=F32)


def _rms_kernel(x_ref, g_ref, o_ref):
    x = x_ref[...]
    ms = jnp.mean(x * x, axis=-1, keepdims=True)
    o_ref[...] = ((x * lax.rsqrt(ms + EPS)) * g_ref[...]).astype(o_ref.dtype)


def rms_norm_rows(x, g):
    m, d = x.shape
    tm = _row_tile(m, 256)
    return pl.pallas_call(
        _rms_kernel,
        grid=(m // tm,),
        in_specs=[pl.BlockSpec((tm, d), lambda i: (i, 0)),
                  pl.BlockSpec((1, d), lambda i: (0, 0))],
        out_specs=pl.BlockSpec((tm, d), lambda i: (i, 0)),
        out_shape=jax.ShapeDtypeStruct((m, d), BF16),
        compiler_params=_params(("arbitrary",), tm * d * 6),
        name="rms_norm",
    )(x, g.reshape(1, d).astype(F32))


def _ffn_in_kernel(x_ref, w_ref, o_ref, *, tf):
    acc = _dot(x_ref[...], w_ref[...])
    gate = acc[:, :tf]
    up = acc[:, tf:]
    o_ref[...] = (gate * jax.nn.sigmoid(gate) * up).astype(o_ref.dtype)


def ffn_in(xn, w_gu, tf):
    m, k = xn.shape
    f = w_gu.shape[1] // 2
    tm = _row_tile(m, 1024)
    blk = tm * k * 2 + k * 2 * tf * 2 + tm * tf * 2 + tm * 2 * tf * 4
    return pl.pallas_call(
        functools.partial(_ffn_in_kernel, tf=tf),
        grid=(m // tm, f // tf),
        in_specs=[pl.BlockSpec((tm, k), lambda i, j: (i, 0)),
                  pl.BlockSpec((k, 2 * tf), lambda i, j: (0, j))],
        out_specs=pl.BlockSpec((tm, tf), lambda i, j: (i, j)),
        out_shape=jax.ShapeDtypeStruct((m, f), BF16),
        compiler_params=_params(("arbitrary", "arbitrary"), blk),
        name="ffn_in",
    )(xn, w_gu)


def _ffn_tile(f):
    for tf in (256, 128):
        if f % tf == 0:
            return tf
    raise ValueError(f"D_FF={f} must be a multiple of {V7X_LANES}")


def pack_gate_up(w_in, tf):
    k, f2 = w_in.shape
    f = f2 // 2
    w = w_in.astype(BF16).reshape(k, 2, f // tf, tf)
    return jnp.swapaxes(w, 1, 2).reshape(k, f2)


def _mm_res_kernel(a_ref, w_ref, r_ref, o_ref, *, scale):
    acc = _dot(a_ref[...], w_ref[...])
    o_ref[...] = r_ref[...] + scale * acc


def matmul_residual(a, w, res, scale, tm_want):
    m, k = a.shape
    n = w.shape[1]
    tm = _row_tile(m, tm_want)
    tn = _row_tile(n, 512)
    blk = tm * k * 2 + k * tn * 2 + 3 * tm * tn * 4
    return pl.pallas_call(
        functools.partial(_mm_res_kernel, scale=scale),
        grid=(m // tm, n // tn),
        in_specs=[pl.BlockSpec((tm, k), lambda i, j: (i, 0)),
                  pl.BlockSpec((k, tn), lambda i, j: (0, j)),
                  pl.BlockSpec((tm, tn), lambda i, j: (i, j))],
        out_specs=pl.BlockSpec((tm, tn), lambda i, j: (i, j)),
        out_shape=jax.ShapeDtypeStruct((m, n), F32),
        compiler_params=_params(("arbitrary", "arbitrary"), blk),
        name="matmul_residual",
    )(a, w, res)


def _head_rms(acc, hd):
    tn = acc.shape[1]
    parts = []
    if hd == V7X_LANES:
        for h in range(tn // hd):
            xh = acc[:, h * hd:(h + 1) * hd]
            ms = jnp.mean(xh * xh, axis=-1, keepdims=True)
            parts.append(xh * lax.rsqrt(ms + EPS))
    else:
        assert hd * 2 == V7X_LANES
        lo = lax.broadcasted_iota(jnp.int32, (1, V7X_LANES), 1) < hd
        for h in range(tn // V7X_LANES):
            xh = acc[:, h * V7X_LANES:(h + 1) * V7X_LANES]
            x2 = xh * xh
            s_lo = jnp.sum(jnp.where(lo, x2, 0.0), axis=-1, keepdims=True)
            s_hi = jnp.sum(jnp.where(lo, 0.0, x2), axis=-1, keepdims=True)
            ms = jnp.where(lo, s_lo, s_hi) * (1.0 / hd)
            parts.append(xh * lax.rsqrt(ms + EPS))
    return jnp.concatenate(parts, axis=-1)


def _qkv_kernel(x_ref, w_ref, g_ref, o16_ref, o32_ref, *, hd, n_q_tiles, n_norm_tiles):
    j = pl.program_id(1)
    acc = _dot(x_ref[...], w_ref[...])

    @pl.when(j < n_norm_tiles)
    def _():
        y = _head_rms(acc, hd) * g_ref[...]
        o16_ref[...] = y.astype(o16_ref.dtype)

        @pl.when(j >= n_q_tiles)
        def _():
            o32_ref[...] = y

    @pl.when(j >= n_norm_tiles)
    def _():
        o16_ref[...] = acc.astype(o16_ref.dtype)
        o32_ref[...] = acc


def qkv_project(xn, w, gains, hd, q_width, kv_width):
    m, k = xn.shape
    n = w.shape[1]
    tm = _row_tile(m, 1024)
    tn = _row_tile(math.gcd(q_width, kv_width), 512)
    n_q = q_width // tn
    n_norm = (q_width + kv_width) // tn
    blk = tm * k * 2 + k * tn * 2 + tm * tn * (2 + 4 + 4)
    return pl.pallas_call(
        functools.partial(_qkv_kernel, hd=hd, n_q_tiles=n_q, n_norm_tiles=n_norm),
        grid=(m // tm, n // tn),
        in_specs=[pl.BlockSpec((tm, k), lambda i, j: (i, 0)),
                  pl.BlockSpec((k, tn), lambda i, j: (0, j)),
                  pl.BlockSpec((1, tn), lambda i, j: (0, j))],
        out_specs=[pl.BlockSpec((tm, tn), lambda i, j: (i, j)),
                   pl.BlockSpec((tm, tn), lambda i, j: (i, jnp.maximum(j - n_q, 0)))],
        out_shape=[jax.ShapeDtypeStruct((m, n), BF16),
                   jax.ShapeDtypeStruct((m, 2 * kv_width), F32)],
        compiler_params=_params(("arbitrary", "arbitrary"), blk),
        name="qkv_project",
    )(xn, w, gains)


def _fgate_kernel(x_ref, w_ref, b_ref, o_ref):
    z = _dot(x_ref[...], w_ref[...]) + b_ref[...]
    o_ref[...] = jnp.minimum(z, 0.0) - jnp.log1p(jnp.exp(-jnp.abs(z)))


def forget_gate(xn, w_pad, b_pad):
    m, k = xn.shape
    n = w_pad.shape[1]
    tm = _row_tile(m, 1024)
    blk = tm * k * 2 + k * n * 2 + tm * n * 8
    return pl.pallas_call(
        _fgate_kernel,
        grid=(m // tm,),
        in_specs=[pl.BlockSpec((tm, k), lambda i: (i, 0)),
                  pl.BlockSpec((k, n), lambda i: (0, 0)),
                  pl.BlockSpec((1, n), lambda i: (0, 0))],
        out_specs=pl.BlockSpec((tm, n), lambda i: (i, 0)),
        out_shape=jax.ShapeDtypeStruct((m, n), F32),
        compiler_params=_params(("arbitrary",), blk),
        name="forget_gate",
    )(xn, w_pad, b_pad)


def _cumsum_kernel(x_ref, o_ref, *, blk):
    length, width = x_ref.shape
    row = lax.broadcasted_iota(jnp.int32, (blk, blk), 0)
    col = lax.broadcasted_iota(jnp.int32, (blk, blk), 1)
    tri = (row >= col).astype(F32)

    def body(i, carry):
        start = pl.multiple_of(i * blk, blk)
        xs = x_ref[pl.ds(start, blk), :]
        c = jnp.dot(tri, xs, precision=lax.Precision.HIGHEST, preferred_element_type=F32) + carry
        o_ref[pl.ds(start, blk), :] = c
        return c[blk - 1:blk, :]

    lax.fori_loop(0, length // blk, body, jnp.zeros((1, width), F32))


def cumsum_time(x, blk):
    b, length, width = x.shape
    return pl.pallas_call(
        functools.partial(_cumsum_kernel, blk=blk),
        grid=(b,),
        in_specs=[pl.BlockSpec((None, length, width), lambda i: (i, 0, 0))],
        out_specs=pl.BlockSpec((None, length, width), lambda i: (i, 0, 0)),
        out_shape=jax.ShapeDtypeStruct((b, length, width), F32),
        compiler_params=_params(("arbitrary",), 2 * length * width * 4),
        name="cumsum_time",
    )(x)


def _band_kernel(*refs, n_kblk, kv_heads, group, hd, scale, has_sink, first_blk, heads_per_step):
    q_ref = refs[0]
    k_refs = refs[1:1 + n_kblk]
    v_refs = refs[1 + n_kblk:1 + 2 * n_kblk]
    bias_ref = refs[1 + 2 * n_kblk]
    pos = 2 + 2 * n_kblk
    sink_ref = refs[pos] if has_sink else None
    o_ref = refs[-1]

    hg = pl.program_id(0)
    qi = pl.program_id(1)
    widths = [r.shape[0] for r in k_refs]
    outs = []
    for kv in range(kv_heads):
        kparts = [r[:, kv * hd:(kv + 1) * hd].astype(BF16) for r in k_refs]
        vparts = [r[:, kv * hd:(kv + 1) * hd].astype(BF16) for r in v_refs]
        for g in range(group):
            h = kv * group + g
            qh = q_ref[:, h * hd:(h + 1) * hd]
            sparts = []
            for t, kp in enumerate(kparts):
                st = _dot_nt(qh, kp)
                if first_blk is not None:
                    st = jnp.where(first_blk(qi, t) >= 0, st, NEG_INF)
                sparts.append(st)
            s = jnp.concatenate(sparts, axis=1) if len(sparts) > 1 else sparts[0]
            s = s * scale + bias_ref[h]
            m = jnp.max(s, axis=-1, keepdims=True)
            if has_sink:
                sk = sink_ref[hg * heads_per_step + h]
                m = jnp.maximum(m, sk)
            e = jnp.exp(s - m)
            denom = jnp.sum(e, axis=-1, keepdims=True)
            if has_sink:
                denom = denom + jnp.exp(sk - m)
            p = (e / denom).astype(BF16)
            off = 0
            acc = None
            for w_t, vp in zip(widths, vparts):
                part = _dot(p[:, off:off + w_t], vp)
                acc = part if acc is None else acc + part
                off += w_t
            outs.append(acc)
    per_store = V7X_LANES // hd
    for c in range(len(outs) // per_store):
        piece = outs[c * per_store:(c + 1) * per_store]
        val = jnp.concatenate(piece, axis=1) if per_store > 1 else piece[0]
        o_ref[:, c * V7X_LANES:(c + 1) * V7X_LANES] = val.astype(o_ref.dtype)


def band_attention(q_arr, q_spec, k_arrs, k_specs, v_arrs, v_specs, bias, sinks, o_prev, o_shape, o_spec,
                   grid, *, kv_heads, group, hd, first_blk, block_bytes):
    heads_per_step = kv_heads * group
    n_kblk = len(k_arrs)
    has_sink = sinks is not None
    tq, nk = bias.shape[1], bias.shape[2]
    in_specs = [q_spec, *k_specs, *v_specs, pl.BlockSpec((heads_per_step, tq, nk), lambda a, b: (a, 0, 0))]
    args = [q_arr, *k_arrs, *v_arrs, bias]
    if has_sink:
        in_specs.append(pl.BlockSpec(memory_space=pltpu.SMEM))
        args.append(sinks)
    aliases = {}
    if o_prev is not None:
        in_specs.append(pl.BlockSpec(memory_space=pl.ANY))
        args.append(o_prev)
        aliases = {len(args) - 1: 0}
    kern = functools.partial(_band_kernel, n_kblk=n_kblk, kv_heads=kv_heads, group=group, hd=hd,
                             scale=hd ** -0.5, has_sink=has_sink, first_blk=first_blk,
                             heads_per_step=heads_per_step)
    if o_prev is not None:
        body = lambda *refs: kern(*refs[:-2], refs[-1])
    else:
        body = kern
    return pl.pallas_call(
        body,
        grid=grid,
        in_specs=in_specs,
        out_specs=o_spec,
        out_shape=o_shape,
        input_output_aliases=aliases,
        compiler_params=_params(("arbitrary", "arbitrary"), block_bytes),
        name="band_attention",
    )(*args)


def _fox_kernel(q_ref, k_ref, v_ref, cq_ref, ck_ref, *rest, tq, n_past_blocks, scale):
    o_ref = rest[-1]
    qb = pl.program_id(2)
    q = q_ref[...]
    cq = cq_ref[...]
    hd = q.shape[1]
    n_off = n_past_blocks + qb

    def step(kb, carry, diagonal):
        m, l, acc = carry
        start = pl.multiple_of(kb * tq, tq)
        kblk = k_ref[pl.ds(start, tq), :]
        vblk = v_ref[pl.ds(start, tq), :]
        ck = ck_ref[kb]
        s = _dot_nt(q, kblk) * scale + (cq - ck)
        if diagonal:
            row = lax.broadcasted_iota(jnp.int32, (tq, tq), 0)
            col = lax.broadcasted_iota(jnp.int32, (tq, tq), 1)
            s = jnp.where(col <= row, s, NEG_INF)
        m_new = jnp.maximum(m, jnp.max(s, axis=-1, keepdims=True))
        alpha = jnp.exp(m - m_new)
        p = jnp.exp(s - m_new)
        l_new = alpha * l + jnp.sum(p, axis=-1, keepdims=True)
        acc_new = alpha * acc + _dot(p.astype(BF16), vblk)
        return m_new, l_new, acc_new

    init = (jnp.full((tq, 1), NEG_INF, F32), jnp.zeros((tq, 1), F32), jnp.zeros((tq, hd), F32))
    carry = lax.fori_loop(0, n_off, lambda kb, c: step(kb, c, False), init)
    _, l, acc = step(n_off, carry, True)
    o_ref[...] = (acc / l).astype(o_ref.dtype)


def fox_attention(q_arr, q_row0_blocks, k_arr, v_arr, k_col0, v_col0, cq, ck, o_prev, o_shape, *,
                  n_seq, heads, hd, tq, n_qblk, seq_len, n_past_blocks):
    def q_map(b, h, qb):
        return (q_row0_blocks + b * n_qblk + qb, h)

    in_specs = [pl.BlockSpec((tq, hd), q_map),
                pl.BlockSpec((seq_len, hd), lambda b, h, qb: (b, k_col0 + h)),
                pl.BlockSpec((seq_len, hd), lambda b, h, qb: (b, v_col0 + h)),
                pl.BlockSpec((None, None, tq, 1), lambda b, h, qb: (b, h, qb, 0)),
                pl.BlockSpec((None, None, seq_len // tq, 1, tq), lambda b, h, qb: (b, h, 0, 0, 0))]
    args = [q_arr, k_arr, v_arr, cq, ck]
    aliases = {}
    if o_prev is not None:
        in_specs.append(pl.BlockSpec(memory_space=pl.ANY))
        args.append(o_prev)
        aliases = {len(args) - 1: 0}
    blk = 2 * seq_len * hd * 2 + 2 * tq * hd * 2 + tq * V7X_LANES * 4 + 8 * seq_len * 4 + 6 * tq * tq * 4
    return pl.pallas_call(
        functools.partial(_fox_kernel, tq=tq, n_past_blocks=n_past_blocks, scale=hd ** -0.5),
        grid=(n_seq, heads, n_qblk),
        in_specs=in_specs,
        out_specs=pl.BlockSpec((tq, hd), q_map),
        out_shape=o_shape,
        input_output_aliases=aliases,
        compiler_params=_params(("arbitrary", "arbitrary", "arbitrary"), blk),
        name="fox_attention",
    )(*args)


def _band_valid(q_pos, k_pos, n_prev):
    qc = q_pos[:, None] // CHUNK
    kc = k_pos[None, :] // CHUNK
    return (kc <= qc) & (kc >= qc - n_prev)


def _rel_bias_block(table, q_pos, k_pos, n_prev):
    clip = (table.shape[0] - 1) // 2
    rel = jnp.clip(q_pos[:, None] - k_pos[None, :], -clip, clip) + clip
    bias = table.astype(F32)[rel].transpose(2, 0, 1)
    return jnp.where(_band_valid(q_pos, k_pos, n_prev)[None], bias, NEG_INF)


def _t5_bucket(rel, n_buckets):
    half = n_buckets // 2
    max_exact = half // 2
    n = jnp.abs(rel)
    ratio = jnp.maximum(n, max_exact).astype(F32) / max_exact
    large = max_exact + (jnp.log(ratio) / math.log(T5_MAX_DIST / max_exact) * (half - max_exact)).astype(jnp.int32)
    large = jnp.minimum(large, half - 1)
    return jnp.where(rel < 0, half, 0) + jnp.where(n < max_exact, n, large)


def _t5_bias_block(table, q_pos, k_pos, n_prev):
    bk = _t5_bucket(q_pos[:, None] - k_pos[None, :], table.shape[0])
    bias = table.astype(F32)[bk].transpose(2, 0, 1)
    return jnp.where(_band_valid(q_pos, k_pos, n_prev)[None], bias, NEG_INF)


def _band_mixer(qkv16, n_prompt, n_seq, t_dec, cache_k, cache_v, bias_fn, sinks, *,
                d_q, kv_heads_total, group, hd, n_prev, past_len, kv_per_step, tq):
    m = qkv16.shape[0]
    kvw = kv_heads_total * hd
    qw_step = kv_per_step * group * hd
    kw_step = kv_per_step * hd
    n_hg = kv_heads_total // kv_per_step
    heads_step = kv_per_step * group
    win = n_prev * CHUNK
    tkb = math.gcd(tq, win)
    n_kblk = (tq + win) // tkb
    n_back = win // tkb
    ratio = tq // tkb
    k_col0 = d_q // kw_step
    v_col0 = (d_q + kvw) // kw_step
    o_shape = jax.ShapeDtypeStruct((m, d_q), BF16)

    q_pos = win + jnp.arange(tq)
    k_pos = jnp.arange(tq + win)
    bias_p = bias_fn(q_pos, k_pos, n_prev)

    def first_blk(qi, t):
        return qi * ratio - n_back + t

    def kmap(t, col0):
        return lambda hg, qi: (jnp.maximum(qi * ratio - n_back + t, 0), col0 + hg)

    k_specs = [pl.BlockSpec((tkb, kw_step), kmap(t, k_col0)) for t in range(n_kblk)]
    v_specs = [pl.BlockSpec((tkb, kw_step), kmap(t, v_col0)) for t in range(n_kblk)]
    blk = (tq * qw_step * 2 * 2 + 2 * n_kblk * tkb * kw_step * 2 + heads_step * tq * (tq + win) * 4
           + 8 * tq * (tq + win) * 4)
    o = band_attention(
        qkv16, pl.BlockSpec((tq, qw_step), lambda hg, qi: (qi, hg)),
        [qkv16] * n_kblk, k_specs, [qkv16] * n_kblk, v_specs, bias_p, sinks, None, o_shape,
        pl.BlockSpec((tq, qw_step), lambda hg, qi: (qi, hg)),
        (n_hg, n_prompt // tq), kv_heads=kv_per_step, group=group, hd=hd, first_blk=first_blk, block_bytes=blk)

    w = cache_k.shape[1]
    q_pos = past_len + jnp.arange(t_dec)
    k_pos = past_len - w + jnp.arange(w + t_dec)
    bias_s = bias_fn(q_pos, k_pos, n_prev)
    ck2 = cache_k.reshape(n_seq * w, kvw)
    cv2 = cache_v.reshape(n_seq * w, kvw)
    row0 = n_prompt // t_dec
    k_specs = [pl.BlockSpec((w, kw_step), lambda hg, b: (b, hg)),
               pl.BlockSpec((t_dec, kw_step), lambda hg, b: (row0 + b, k_col0 + hg))]
    v_specs = [pl.BlockSpec((w, kw_step), lambda hg, b: (b, hg)),
               pl.BlockSpec((t_dec, kw_step), lambda hg, b: (row0 + b, v_col0 + hg))]
    blk = (t_dec * qw_step * 4 + 2 * w * kw_step * 4 + 2 * t_dec * kw_step * 2
           + heads_step * t_dec * (w + t_dec) * 4 + 8 * t_dec * (w + t_dec) * 4 + 2 * w * kw_step * 2)
    o = band_attention(
        qkv16, pl.BlockSpec((t_dec, qw_step), lambda hg, b: (row0 + b, hg)),
        [ck2, qkv16], k_specs, [cv2, qkv16], v_specs, bias_s, sinks, o, o_shape,
        pl.BlockSpec((t_dec, qw_step), lambda hg, b: (row0 + b, hg)),
        (n_hg, n_seq), kv_heads=kv_per_step, group=group, hd=hd, first_blk=None, block_bytes=blk)
    return o


def _roll_cache(cache, new_rows, t_dec):
    return jnp.concatenate([cache[:, t_dec:], new_rows.reshape(cache.shape[0], t_dec, *cache.shape[2:])], axis=1)


def kernel(x_prompt, x_sample, cache_a_k, cache_a_v, cache_b_k, cache_b_v, cache_b_logf, cache_c_k, cache_c_v,
           norm_g, ffn_w_in, ffn_w_out, a_w_qkv, a_w_o, a_qk_g, a_rel_bias, b_w_qkvf, b_f_bias, b_w_o, b_qk_g,
           c_w_qkv, c_w_o, c_qk_g, c_sinks, t5_bias):
    batch, seq, d = x_prompt.shape
    n_seq, t_dec, _ = x_sample.shape
    assert batch == 1 and t_dec == CHUNK and seq % 256 == 0
    depth = norm_g.shape[0]
    d_ff = ffn_w_out.shape[2]
    past_len = cache_b_k.shape[2]
    assert past_len % CHUNK == 0
    n_prompt = batch * seq
    m = n_prompt + n_seq * t_dec

    h_a, hd_a = a_rel_bias.shape[2], a_qk_g.shape[2]
    h_b, hd_b = b_f_bias.shape[1], b_qk_g.shape[2]
    h_c, hd_c = c_sinks.shape[1], c_qk_g.shape[2]
    kv_c = cache_c_k.shape[3]
    g_c = h_c // kv_c
    nb_c = cache_c_k.shape[2] // CHUNK
    assert hd_a == V7X_LANES and hd_b == V7X_LANES

    x = jnp.concatenate([x_prompt.reshape(n_prompt, d), x_sample.reshape(n_seq * t_dec, d)], axis=0)
    tf = _ffn_tile(d_ff)

    def half_ffn(x, g, w_in, w_out):
        xn = rms_norm_rows(x, g)
        h = ffn_in(xn, pack_gate_up(w_in, tf), tf)
        return matmul_residual(h, w_out.astype(BF16), x, FFN_RES, 512)

    a_kp, a_vp, a_ks, a_vs = [], [], [], []
    b_kp, b_vp, b_lp, b_ks, b_vs, b_ls = [], [], [], [], [], []
    c_kp, c_vp, c_ks, c_vs = [], [], [], []

    for i in range(depth):
        kind, j = i % 3, i // 3
        x = half_ffn(x, norm_g[i, 0], ffn_w_in[i, 0], ffn_w_out[i, 0])
        u = rms_norm_rows(x, norm_g[i, 1])
        if kind == 0:
            dq = h_a * hd_a
            gains = jnp.concatenate([jnp.tile(a_qk_g[j, 0], h_a), jnp.tile(a_qk_g[j, 1], h_a),
                                     jnp.ones((dq,), F32)]).reshape(1, 3 * dq).astype(F32)
            qkv16, kv32 = qkv_project(u, a_w_qkv[j].astype(BF16), gains, hd_a, dq, dq)
            o = _band_mixer(qkv16, n_prompt, n_seq, t_dec, cache_a_k[j], cache_a_v[j],
                            functools.partial(_rel_bias_block, a_rel_bias[j]), None,
                            d_q=dq, kv_heads_total=h_a, group=1, hd=hd_a, n_prev=NB_A, past_len=past_len,
                            kv_per_step=4, tq=256)
            x = matmul_residual(o, a_w_o[j].astype(BF16), x, 1.0, 1024)
            w = min(NB_A * CHUNK, seq)
            k32, v32 = kv32[:, :dq], kv32[:, dq:]
            a_kp.append(k32[n_prompt - w:n_prompt].reshape(batch, w, h_a, hd_a))
            a_vp.append(v32[n_prompt - w:n_prompt].reshape(batch, w, h_a, hd_a))
            a_ks.append(_roll_cache(cache_a_k[j], k32[n_prompt:].reshape(n_seq, t_dec, h_a, hd_a), t_dec))
            a_vs.append(_roll_cache(cache_a_v[j], v32[n_prompt:].reshape(n_seq, t_dec, h_a, hd_a), t_dec))
        elif kind == 1:
            dq = h_b * hd_b
            gains = jnp.concatenate([jnp.tile(b_qk_g[j, 0], h_b), jnp.tile(b_qk_g[j, 1], h_b),
                                     jnp.ones((dq,), F32)]).reshape(1, 3 * dq).astype(F32)
            w16 = b_w_qkvf[j].astype(BF16)
            qkv16, kv32 = qkv_project(u, w16[:, :3 * dq], gains, hd_b, dq, dq)
            w_f = jnp.pad(w16[:, 3 * dq:], ((0, 0), (0, V7X_LANES - h_b)))
            b_f = jnp.pad(b_f_bias[j].astype(F32), (0, V7X_LANES - h_b)).reshape(1, V7X_LANES)
            logf = forget_gate(u, w_f, b_f)
            k32, v32 = kv32[:, :dq], kv32[:, dq:]
            cum_p = cumsum_time(logf[:n_prompt].reshape(batch, seq, V7X_LANES), 256)[:, :, :h_b]
            cq = cum_p.transpose(0, 2, 1)[..., None]
            ck = cum_p.transpose(0, 2, 1).reshape(batch, h_b, seq // 256, 1, 256)
            o_shape = jax.ShapeDtypeStruct((m, dq), BF16)
            tq = 256
            o = fox_attention(qkv16, 0, qkv16, qkv16, h_b, 2 * h_b, cq, ck, None, o_shape,
                              n_seq=batch, heads=h_b, hd=hd_b, tq=tq, n_qblk=seq // tq, seq_len=seq,
                              n_past_blocks=0)
            new_k16 = qkv16[n_prompt:, dq:2 * dq].reshape(n_seq, t_dec, dq)
            new_v16 = qkv16[n_prompt:, 2 * dq:].reshape(n_seq, t_dec, dq)
            k_all = jnp.concatenate([cache_b_k[j].reshape(n_seq, past_len, dq).astype(BF16), new_k16], axis=1)
            v_all = jnp.concatenate([cache_b_v[j].reshape(n_seq, past_len, dq).astype(BF16), new_v16], axis=1)
            ls = past_len + t_dec
            logf_s = logf[n_prompt:].reshape(n_seq, t_dec, V7X_LANES)
            logf_all = jnp.concatenate(
                [jnp.pad(cache_b_logf[j].astype(F32), ((0, 0), (0, 0), (0, V7X_LANES - h_b))), logf_s], axis=1)
            cum_s = cumsum_time(logf_all, CHUNK)[:, :, :h_b]
            cq = cum_s[:, past_len:].transpose(0, 2, 1)[..., None]
            ck = cum_s.transpose(0, 2, 1).reshape(n_seq, h_b, ls // t_dec, 1, t_dec)
            o = fox_attention(qkv16, n_prompt // t_dec, k_all.reshape(n_seq * ls, dq), v_all.reshape(n_seq * ls, dq),
                              0, 0, cq, ck, o, o_shape,
                              n_seq=n_seq, heads=h_b, hd=hd_b, tq=t_dec, n_qblk=1, seq_len=ls,
                              n_past_blocks=past_len // t_dec)
            x = matmul_residual(o, b_w_o[j].astype(BF16), x, 1.0, 1024)
            b_kp.append(k32[:n_prompt].reshape(batch, seq, h_b, hd_b))
            b_vp.append(v32[:n_prompt].reshape(batch, seq, h_b, hd_b))
            b_lp.append(logf[:n_prompt, :h_b].reshape(batch, seq, h_b))
            b_ks.append(k32[n_prompt:].reshape(n_seq, t_dec, h_b, hd_b))
            b_vs.append(v32[n_prompt:].reshape(n_seq, t_dec, h_b, hd_b))
            b_ls.append(logf[n_prompt:, :h_b].reshape(n_seq, t_dec, h_b))
        else:
            dq, kvw = h_c * hd_c, kv_c * hd_c
            gains = jnp.concatenate([jnp.tile(c_qk_g[j, 0], h_c), jnp.tile(c_qk_g[j, 1], kv_c),
                                     jnp.ones((kvw,), F32)]).reshape(1, dq + 2 * kvw).astype(F32)
            qkv16, kv32 = qkv_project(u, c_w_qkv[j].astype(BF16), gains, hd_c, dq, kvw)
            o = _band_mixer(qkv16, n_prompt, n_seq, t_dec, cache_c_k[j], cache_c_v[j],
                            functools.partial(_t5_bias_block, t5_bias), c_sinks[j].astype(F32),
                            d_q=dq, kv_heads_total=kv_c, group=g_c, hd=hd_c, n_prev=nb_c, past_len=past_len,
                            kv_per_step=2, tq=256)
            x = matmul_residual(o, c_w_o[j].astype(BF16), x, 1.0, 1024)
            w = min(nb_c * CHUNK, seq)
            k32, v32 = kv32[:, :kvw], kv32[:, kvw:]
            c_kp.append(k32[n_prompt - w:n_prompt].reshape(batch, w, kv_c, hd_c))
            c_vp.append(v32[n_prompt - w:n_prompt].reshape(batch, w, kv_c, hd_c))
            c_ks.append(_roll_cache(cache_c_k[j], k32[n_prompt:].reshape(n_seq, t_dec, kv_c, hd_c), t_dec))
            c_vs.append(_roll_cache(cache_c_v[j], v32[n_prompt:].reshape(n_seq, t_dec, kv_c, hd_c), t_dec))
        x = half_ffn(x, norm_g[i, 2], ffn_w_in[i, 1], ffn_w_out[i, 1])

    y_prompt = x[:n_prompt].reshape(batch, seq, d)
    y_sample = x[n_prompt:].reshape(n_seq, t_dec, d)
    return (y_prompt, y_sample,
            jnp.stack(a_kp), jnp.stack(a_vp), jnp.stack(a_ks), jnp.stack(a_vs),
            jnp.stack(b_kp), jnp.stack(b_vp), jnp.stack(b_lp), jnp.stack(b_ks), jnp.stack(b_vs), jnp.stack(b_ls),
            jnp.stack(c_kp), jnp.stack(c_vp), jnp.stack(c_ks), jnp.stack(c_vs))
```

```python
import functools
import math

import jax
import jax.numpy as jnp
from jax import lax
from jax.experimental import pallas as pl
from jax.experimental.pallas import tpu as pltpu

CHUNK = 64
NB_A = 8
EPS = 1e-6
NEG_INF = -1e30
FFN_RES = 0.5
T5_MAX_DIST = 128
LOG2E = math.log2(math.e)

V7X_LANES = 128
V7X_VMEM_BYTES = 64 * 1024 * 1024
VMEM_HEADROOM_BYTES = 14 * 1024 * 1024

F32 = jnp.float32
BF16 = jnp.bfloat16


def _params(semantics, block_bytes):
    limit = min(2 * block_bytes + VMEM_HEADROOM_BYTES, V7X_VMEM_BYTES - 2 * 1024 * 1024)
    return pltpu.CompilerParams(dimension_semantics=semantics, vmem_limit_bytes=int(limit))


def _row_tile(m, want):
    t = want
    while m % t:
        t //= 2
    return t


def _dot(a, b):
    return jnp.dot(a, b, preferred_element_type=F32)


def _dot_nt(a, b):
    return lax.dot_general(a, b, (((1,), (1,)), ((), ())), preferred_element_type=F32)


def _rms_kernel(x_ref, g_ref, o_ref):
    x = x_ref[...]
    ms = jnp.mean(x * x, axis=-1, keepdims=True)
    o_ref[...] = ((x * lax.rsqrt(ms + EPS)) * g_ref[...]).astype(o_ref.dtype)


def rms_norm_rows(x, g):
    m, d = x.shape
    tm = _row_tile(m, 256)
    return pl.pallas_call(
        _rms_kernel,
        grid=(m // tm,),
        in_specs=[pl.BlockSpec((tm, d), lambda i: (i, 0)),
                  pl.BlockSpec((1, d), lambda i: (0, 0))],
        out_specs=pl.BlockSpec((tm, d), lambda i: (i, 0)),
        out_shape=jax.ShapeDtypeStruct((m, d), BF16),
        compiler_params=_params(("arbitrary",), tm * d * 6),
        name="rms_norm",
    )(x, g.reshape(1, d).astype(F32))


def _ffn_in_kernel(x_ref, wg_ref, wu_ref, o_ref):
    x = x_ref[...]
    gate = _dot(x, wg_ref[...])
    up = _dot(x, wu_ref[...])
    o_ref[...] = (gate * jax.nn.sigmoid(gate) * up).astype(o_ref.dtype)


def _ffn_tile(f):
    for tf in (256, 128):
        if f % tf == 0:
            return tf
    raise ValueError(f"D_FF={f} must be a multiple of {V7X_LANES}")


def _weight_spec(lead, k, tn, col):
    return pl.BlockSpec((None,) * len(lead) + (k, tn), lambda i, j: (*lead, 0, col(j)))


def ffn_in(xn, w_in16, lead):
    m, k = xn.shape
    f = w_in16.shape[-1] // 2
    tf = _ffn_tile(f)
    n_f = f // tf
    tm = _row_tile(m, 1024)
    blk = tm * k * 2 + 2 * k * tf * 2 + tm * tf * 2 + tm * 2 * tf * 4
    return pl.pallas_call(
        _ffn_in_kernel,
        grid=(m // tm, n_f),
        in_specs=[pl.BlockSpec((tm, k), lambda i, j: (i, 0)),
                  _weight_spec(lead, k, tf, lambda j: j),
                  _weight_spec(lead, k, tf, lambda j: n_f + j)],
        out_specs=pl.BlockSpec((tm, tf), lambda i, j: (i, j)),
        out_shape=jax.ShapeDtypeStruct((m, f), BF16),
        compiler_params=_params(("arbitrary", "arbitrary"), blk),
        name="ffn_in",
    )(xn, w_in16, w_in16)


def _mm_res_kernel(a_ref, w_ref, r_ref, o_ref, *, scale):
    acc = _dot(a_ref[...], w_ref[...])
    o_ref[...] = r_ref[...] + scale * acc


def matmul_residual(a, w, lead, res, scale, tm_want):
    m, k = a.shape
    n = w.shape[-1]
    tm = _row_tile(m, tm_want)
    tn = _row_tile(n, 512)
    blk = tm * k * 2 + k * tn * 2 + 3 * tm * tn * 4
    return pl.pallas_call(
        functools.partial(_mm_res_kernel, scale=scale),
        grid=(m // tm, n // tn),
        in_specs=[pl.BlockSpec((tm, k), lambda i, j: (i, 0)),
                  _weight_spec(lead, k, tn, lambda j: j),
                  pl.BlockSpec((tm, tn), lambda i, j: (i, j))],
        out_specs=pl.BlockSpec((tm, tn), lambda i, j: (i, j)),
        out_shape=jax.ShapeDtypeStruct((m, n), F32),
        compiler_params=_params(("arbitrary", "arbitrary"), blk),
        name="matmul_residual",
    )(a, w, res)


def _head_rms(acc, hd):
    tn = acc.shape[1]
    parts = []
    if hd == V7X_LANES:
        for h in range(tn // hd):
            xh = acc[:, h * hd:(h + 1) * hd]
            ms = jnp.mean(xh * xh, axis=-1, keepdims=True)
            parts.append(xh * lax.rsqrt(ms + EPS))
    else:
        assert hd * 2 == V7X_LANES
        lo = lax.broadcasted_iota(jnp.int32, (1, V7X_LANES), 1) < hd
        for h in range(tn // V7X_LANES):
            xh = acc[:, h * V7X_LANES:(h + 1) * V7X_LANES]
            x2 = xh * xh
            s_lo = jnp.sum(jnp.where(lo, x2, 0.0), axis=-1, keepdims=True)
            s_hi = jnp.sum(jnp.where(lo, 0.0, x2), axis=-1, keepdims=True)
            ms = jnp.where(lo, s_lo, s_hi) * (1.0 / hd)
            parts.append(xh * lax.rsqrt(ms + EPS))
    return jnp.concatenate(parts, axis=-1)


def _qkv_kernel(x_ref, w_ref, g_ref, o16_ref, k32_ref, v32_ref, *, hd, n_q_tiles, n_k_tiles):
    j = pl.program_id(1)
    acc = _dot(x_ref[...], w_ref[...])

    @pl.when(j < n_q_tiles + n_k_tiles)
    def _():
        y = _head_rms(acc, hd) * g_ref[...]
        o16_ref[...] = y.astype(o16_ref.dtype)

        @pl.when(j >= n_q_tiles)
        def _():
            k32_ref[...] = y

    @pl.when(j >= n_q_tiles + n_k_tiles)
    def _():
        o16_ref[...] = acc.astype(o16_ref.dtype)
        v32_ref[...] = acc


def qkv_project(xn, row0, rows, w, lead, gains, hd, q_width, kv_width):
    k = xn.shape[1]
    n = q_width + 2 * kv_width
    tm = _row_tile(math.gcd(row0, rows) if row0 else rows, 1024)
    tn = _row_tile(math.gcd(q_width, kv_width), 512)
    n_q, n_k = q_width // tn, kv_width // tn
    i0 = row0 // tm
    blk = tm * k * 2 + k * tn * 2 + tm * tn * (2 + 4 + 4)
    return pl.pallas_call(
        functools.partial(_qkv_kernel, hd=hd, n_q_tiles=n_q, n_k_tiles=n_k),
        grid=(rows // tm, n // tn),
        in_specs=[pl.BlockSpec((tm, k), lambda i, j: (i0 + i, 0)),
                  _weight_spec(lead, k, tn, lambda j: j),
                  pl.BlockSpec((1, tn), lambda i, j: (0, j))],
        out_specs=[pl.BlockSpec((tm, tn), lambda i, j: (i, j)),
                   pl.BlockSpec((tm, tn), lambda i, j: (i, jnp.clip(j - n_q, 0, n_k - 1))),
                   pl.BlockSpec((tm, tn), lambda i, j: (i, jnp.clip(j - n_q - n_k, 0, n_k - 1)))],
        out_shape=[jax.ShapeDtypeStruct((rows, n), BF16),
                   jax.ShapeDtypeStruct((rows, kv_width), F32),
                   jax.ShapeDtypeStruct((rows, kv_width), F32)],
        compiler_params=_params(("arbitrary", "arbitrary"), blk),
        name="qkv_project",
    )(xn, w, gains)


def _fgate_kernel(x_ref, w_ref, b_ref, o_ref):
    z = _dot(x_ref[...], w_ref[...]) + b_ref[...]
    o_ref[...] = jnp.minimum(z, 0.0) - jnp.log1p(jnp.exp(-jnp.abs(z)))


def forget_gate(xn, w_pad, b_pad):
    m, k = xn.shape
    n = w_pad.shape[1]
    tm = _row_tile(m, 1024)
    blk = tm * k * 2 + k * n * 2 + tm * n * 8
    return pl.pallas_call(
        _fgate_kernel,
        grid=(m // tm,),
        in_specs=[pl.BlockSpec((tm, k), lambda i: (i, 0)),
                  pl.BlockSpec((k, n), lambda i: (0, 0)),
                  pl.BlockSpec((1, n), lambda i: (0, 0))],
        out_specs=pl.BlockSpec((tm, n), lambda i: (i, 0)),
        out_shape=jax.ShapeDtypeStruct((m, n), F32),
        compiler_params=_params(("arbitrary",), blk),
        name="forget_gate",
    )(xn, w_pad, b_pad)


def _cumsum_kernel(x_ref, o_ref, *, blk):
    length, width = x_ref.shape
    row = lax.broadcasted_iota(jnp.int32, (blk, blk), 0)
    col = lax.broadcasted_iota(jnp.int32, (blk, blk), 1)
    tri = (row >= col).astype(F32)

    def body(i, carry):
        start = pl.multiple_of(i * blk, blk)
        xs = x_ref[pl.ds(start, blk), :]
        c = jnp.dot(tri, xs, precision=lax.Precision.HIGHEST, preferred_element_type=F32) + carry
        o_ref[pl.ds(start, blk), :] = c
        return c[blk - 1:blk, :]

    lax.fori_loop(0, length // blk, body, jnp.zeros((1, width), F32))


def cumsum_time(x, n_seq, length, blk):
    width = x.shape[1]
    return pl.pallas_call(
        functools.partial(_cumsum_kernel, blk=blk),
        grid=(n_seq,),
        in_specs=[pl.BlockSpec((length, width), lambda i: (i, 0))],
        out_specs=pl.BlockSpec((None, length, width), lambda i: (i, 0, 0)),
        out_shape=jax.ShapeDtypeStruct((n_seq, length, width), F32),
        compiler_params=_params(("arbitrary",), 2 * length * width * 4),
        name="cumsum_time",
    )(x)


def _band_kernel(*refs, n_kblk, kv_heads, group, hd, scale2, has_sink, first_blk, heads_per_step):
    q_ref = refs[0]
    k_refs = refs[1:1 + n_kblk]
    v_refs = refs[1 + n_kblk:1 + 2 * n_kblk]
    bias_ref = refs[1 + 2 * n_kblk]
    sink_ref = refs[2 + 2 * n_kblk] if has_sink else None
    o_ref = refs[-1]

    hg = pl.program_id(0)
    qi = pl.program_id(1)
    widths = [r.shape[0] for r in k_refs]
    outs = []
    for kv in range(kv_heads):
        kparts = [r[:, kv * hd:(kv + 1) * hd].astype(BF16) for r in k_refs]
        vparts = [r[:, kv * hd:(kv + 1) * hd].astype(BF16) for r in v_refs]
        for g in range(group):
            h = kv * group + g
            qh = q_ref[:, h * hd:(h + 1) * hd]
            sparts = []
            for t, kp in enumerate(kparts):
                st = _dot_nt(qh, kp)
                if first_blk is not None:
                    st = jnp.where(first_blk(qi, t) >= 0, st, NEG_INF)
                sparts.append(st)
            s = jnp.concatenate(sparts, axis=1) if len(sparts) > 1 else sparts[0]
            z = s * scale2 + bias_ref[h]
            m = jnp.max(z, axis=-1, keepdims=True)
            if has_sink:
                sk = sink_ref[hg * heads_per_step + h] * LOG2E
                m = jnp.maximum(m, sk)
            e = jnp.exp2(z - m)
            denom = jnp.sum(e, axis=-1, keepdims=True)
            if has_sink:
                denom = denom + jnp.exp2(sk - m)
            p = (e / denom).astype(BF16)
            off = 0
            acc = None
            for w_t, vp in zip(widths, vparts):
                part = _dot(p[:, off:off + w_t], vp)
                acc = part if acc is None else acc + part
                off += w_t
            outs.append(acc)
    per_store = V7X_LANES // hd
    for c in range(len(outs) // per_store):
        piece = outs[c * per_store:(c + 1) * per_store]
        val = jnp.concatenate(piece, axis=1) if per_store > 1 else piece[0]
        o_ref[:, c * V7X_LANES:(c + 1) * V7X_LANES] = val.astype(o_ref.dtype)


def _drop_alias_ref(kern):
    def body(*refs):
        return kern(*refs[:-2], refs[-1])
    return body


def band_attention(q_arr, q_spec, k_arrs, k_specs, v_arrs, v_specs, bias, sinks, o_prev, o_shape, o_spec,
                   grid, *, kv_heads, group, hd, first_blk, block_bytes):
    heads_per_step = kv_heads * group
    n_kblk = len(k_arrs)
    has_sink = sinks is not None
    tq, nk = bias.shape[1], bias.shape[2]
    in_specs = [q_spec, *k_specs, *v_specs, pl.BlockSpec((heads_per_step, tq, nk), lambda a, b: (a, 0, 0))]
    args = [q_arr, *k_arrs, *v_arrs, bias]
    if has_sink:
        in_specs.append(pl.BlockSpec(memory_space=pltpu.SMEM))
        args.append(sinks)
    body = functools.partial(_band_kernel, n_kblk=n_kblk, kv_heads=kv_heads, group=group, hd=hd,
                             scale2=hd ** -0.5 * LOG2E, has_sink=has_sink, first_blk=first_blk,
                             heads_per_step=heads_per_step)
    aliases = {}
    if o_prev is not None:
        in_specs.append(pl.BlockSpec(memory_space=pl.ANY))
        args.append(o_prev)
        aliases = {len(args) - 1: 0}
        body = _drop_alias_ref(body)
    return pl.pallas_call(
        body,
        grid=grid,
        in_specs=in_specs,
        out_specs=o_spec,
        out_shape=o_shape,
        input_output_aliases=aliases,
        compiler_params=_params(("arbitrary", "arbitrary"), block_bytes),
        name="band_attention",
    )(*args)


def _head_column(cum_blk, h):
    lane = lax.broadcasted_iota(jnp.int32, (1, V7X_LANES), 1)
    return jnp.sum(jnp.where(lane == h, cum_blk, 0.0), axis=-1, keepdims=True)


def _fox_prompt_kernel(q_ref, k_ref, v_ref, cum_ref, ck_ref, o_ref, *, tq, scale2):
    h = pl.program_id(0)
    qb = pl.program_id(1)
    q = q_ref[...]
    hd = q.shape[1]
    cq = _head_column(cum_ref[...], h) * LOG2E

    def step(kb, carry, diagonal):
        m, l, acc = carry
        start = pl.multiple_of(kb * tq, tq)
        kblk = k_ref[pl.ds(start, tq), :]
        vblk = v_ref[pl.ds(start, tq), :]
        ck = ck_ref[kb] * LOG2E
        z = _dot_nt(q, kblk) * scale2 + (cq - ck)
        if diagonal:
            row = lax.broadcasted_iota(jnp.int32, (tq, tq), 0)
            col = lax.broadcasted_iota(jnp.int32, (tq, tq), 1)
            z = jnp.where(col <= row, z, NEG_INF)
        m_new = jnp.maximum(m, jnp.max(z, axis=-1, keepdims=True))
        alpha = jnp.exp2(m - m_new)
        p = jnp.exp2(z - m_new)
        l_new = alpha * l + jnp.sum(p, axis=-1, keepdims=True)
        acc_new = alpha * acc + _dot(p.astype(BF16), vblk)
        return m_new, l_new, acc_new

    carry = (jnp.full((tq, 1), NEG_INF, F32), jnp.zeros((tq, 1), F32), jnp.zeros((tq, hd), F32))
    carry = lax.fori_loop(0, qb, lambda kb, c: step(kb, c, False), carry)
    _, l, acc = step(qb, carry, True)
    o_ref[...] = (acc / l).astype(o_ref.dtype)


def fox_prompt(qkv16, cum, ck, o_shape, *, heads, hd, seq, tq):
    blk = 2 * seq * hd * 2 + 2 * tq * hd * 2 + tq * V7X_LANES * 4 + 8 * seq * 4
    return pl.pallas_call(
        functools.partial(_fox_prompt_kernel, tq=tq, scale2=hd ** -0.5 * LOG2E),
        grid=(heads, seq // tq),
        in_specs=[pl.BlockSpec((tq, hd), lambda h, qb: (qb, h)),
                  pl.BlockSpec((seq, hd), lambda h, qb: (0, heads + h)),
                  pl.BlockSpec((seq, hd), lambda h, qb: (0, 2 * heads + h)),
                  pl.BlockSpec((None, tq, V7X_LANES), lambda h, qb: (0, qb, 0)),
                  pl.BlockSpec((None, seq // tq, 1, tq), lambda h, qb: (h, 0, 0, 0))],
        out_specs=pl.BlockSpec((tq, hd), lambda h, qb: (qb, h)),
        out_shape=o_shape,
        compiler_params=pltpu.CompilerParams(dimension_semantics=("arbitrary", "arbitrary"),
                                             vmem_limit_bytes=int(min(2 * blk + 32 * 1024 * 1024,
                                                                      V7X_VMEM_BYTES - 2 * 1024 * 1024))),
        name="fox_prompt",
    )(qkv16, qkv16, qkv16, cum, ck)


def _fox_sample_kernel(q_ref, kc_ref, vc_ref, kn_ref, vn_ref, cum_ref, ck_ref, prev_ref, o_ref, *,
                       heads_per_step, hd, past, scale2):
    del prev_ref
    hg = pl.program_id(0)
    t = q_ref.shape[0]
    row = lax.broadcasted_iota(jnp.int32, (t, past + t), 0)
    col = lax.broadcasted_iota(jnp.int32, (t, past + t), 1)
    causal = col <= row + past
    for hh in range(heads_per_step):
        sl = slice(hh * hd, (hh + 1) * hd)
        qh = q_ref[:, sl]
        cq = _head_column(cum_ref[...], hg * heads_per_step + hh) * LOG2E
        ck = ck_ref[hh] * LOG2E
        s = jnp.concatenate([_dot_nt(qh, kc_ref[:, sl].astype(BF16)), _dot_nt(qh, kn_ref[:, sl])], axis=1)
        z = jnp.where(causal, s * scale2 + (cq - ck), NEG_INF)
        m = jnp.max(z, axis=-1, keepdims=True)
        e = jnp.exp2(z - m)
        p = (e / jnp.sum(e, axis=-1, keepdims=True)).astype(BF16)
        o = _dot(p[:, :past], vc_ref[:, sl].astype(BF16)) + _dot(p[:, past:], vn_ref[:, sl])
        o_ref[:, sl] = o.astype(o_ref.dtype)


def fox_sample(qkv16_s, cache_k, cache_v, cum, ck, o_prev, row0_blocks, *, n_seq, heads, hd, t_dec, past):
    hps = 4
    width = hps * hd
    n_hg = heads // hps
    ls = past + t_dec
    blk = (2 * past * width * 4 + 3 * t_dec * width * 2 + t_dec * V7X_LANES * 4 + hps * 8 * ls * 4
           + 2 * past * width * 2 + 6 * t_dec * ls * 4)
    return pl.pallas_call(
        functools.partial(_fox_sample_kernel, heads_per_step=hps, hd=hd, past=past, scale2=hd ** -0.5 * LOG2E),
        grid=(n_hg, n_seq),
        in_specs=[pl.BlockSpec((t_dec, width), lambda hg, b: (b, hg)),
                  pl.BlockSpec((past, width), lambda hg, b: (b, hg)),
                  pl.BlockSpec((past, width), lambda hg, b: (b, hg)),
                  pl.BlockSpec((t_dec, width), lambda hg, b: (b, n_hg + hg)),
                  pl.BlockSpec((t_dec, width), lambda hg, b: (b, 2 * n_hg + hg)),
                  pl.BlockSpec((None, t_dec, V7X_LANES), lambda hg, b: (b, past // t_dec, 0)),
                  pl.BlockSpec((None, hps, 1, ls), lambda hg, b: (b, hg, 0, 0)),
                  pl.BlockSpec(memory_space=pl.ANY)],
        out_specs=pl.BlockSpec((t_dec, width), lambda hg, b: (row0_blocks + b, hg)),
        out_shape=jax.ShapeDtypeStruct(o_prev.shape, o_prev.dtype),
        input_output_aliases={7: 0},
        compiler_params=_params(("arbitrary", "arbitrary"), blk),
        name="fox_sample",
    )(qkv16_s, cache_k, cache_v, qkv16_s, qkv16_s, cum, ck, o_prev)


def _toeplitz(vals, tq, nk):
    w = tq + nk
    flat = jnp.tile(vals, (1, tq))[:, :tq * w]
    return flat.reshape(vals.shape[0], tq, w)[:, :, :nk]


def _band_bias(rel_to_index, table, q0, k0, tq, nk, n_prev):
    d = jnp.arange(tq + nk + 1)
    j_minus_i = jnp.where(d < nk, d, d - (tq + nk + 1))
    vals = table.astype(F32)[rel_to_index((q0 - k0) - j_minus_i)].T * LOG2E
    qc = (q0 + jnp.arange(tq))[:, None] // CHUNK
    kc = (k0 + jnp.arange(nk))[None, :] // CHUNK
    valid = (kc <= qc) & (kc >= qc - n_prev)
    return jnp.where(valid[None], _toeplitz(vals, tq, nk), NEG_INF)


def _clip_index(clip, rel):
    return jnp.clip(rel, -clip, clip) + clip


def _t5_bucket(n_buckets, rel):
    half = n_buckets // 2
    max_exact = half // 2
    n = jnp.abs(rel)
    ratio = jnp.maximum(n, max_exact).astype(F32) / max_exact
    large = max_exact + (jnp.log(ratio) / math.log(T5_MAX_DIST / max_exact) * (half - max_exact)).astype(jnp.int32)
    large = jnp.minimum(large, half - 1)
    return jnp.where(rel < 0, half, 0) + jnp.where(n < max_exact, n, large)


def _band_mixer(qkv16_p, qkv16_s, m_total, n_seq, t_dec, cache_k, cache_v, rel_to_index, table, sinks, *,
                d_q, kv_heads_total, group, hd, n_prev, past_len, kv_per_step, tq):
    n_prompt = qkv16_p.shape[0]
    kvw = kv_heads_total * hd
    qw_step = kv_per_step * group * hd
    kw_step = kv_per_step * hd
    n_hg = kv_heads_total // kv_per_step
    heads_step = kv_per_step * group
    win = n_prev * CHUNK
    tkb = math.gcd(tq, win)
    n_kblk = (tq + win) // tkb
    n_back = win // tkb
    ratio = tq // tkb
    k_col0 = d_q // kw_step
    v_col0 = (d_q + kvw) // kw_step
    o_shape = jax.ShapeDtypeStruct((m_total, d_q), BF16)

    bias_p = _band_bias(rel_to_index, table, win, 0, tq, tq + win, n_prev)

    def first_blk(qi, t):
        return qi * ratio - n_back + t

    def kmap(t, col0):
        return lambda hg, qi: (jnp.maximum(qi * ratio - n_back + t, 0), col0 + hg)

    k_specs = [pl.BlockSpec((tkb, kw_step), kmap(t, k_col0)) for t in range(n_kblk)]
    v_specs = [pl.BlockSpec((tkb, kw_step), kmap(t, v_col0)) for t in range(n_kblk)]
    blk = (tq * qw_step * 2 * 2 + 2 * n_kblk * tkb * kw_step * 2 + heads_step * tq * (tq + win) * 4
           + 8 * tq * (tq + win) * 4)
    o = band_attention(
        qkv16_p, pl.BlockSpec((tq, qw_step), lambda hg, qi: (qi, hg)),
        [qkv16_p] * n_kblk, k_specs, [qkv16_p] * n_kblk, v_specs, bias_p, sinks, None, o_shape,
        pl.BlockSpec((tq, qw_step), lambda hg, qi: (qi, hg)),
        (n_hg, n_prompt // tq), kv_heads=kv_per_step, group=group, hd=hd, first_blk=first_blk, block_bytes=blk)

    w = cache_k.shape[1]
    bias_s = _band_bias(rel_to_index, table, past_len, past_len - w, t_dec, w + t_dec, n_prev)
    ck2 = cache_k.reshape(n_seq * w, kvw)
    cv2 = cache_v.reshape(n_seq * w, kvw)
    row0 = n_prompt // t_dec
    k_specs = [pl.BlockSpec((w, kw_step), lambda hg, b: (b, hg)),
               pl.BlockSpec((t_dec, kw_step), lambda hg, b: (b, k_col0 + hg))]
    v_specs = [pl.BlockSpec((w, kw_step), lambda hg, b: (b, hg)),
               pl.BlockSpec((t_dec, kw_step), lambda hg, b: (b, v_col0 + hg))]
    blk = (t_dec * qw_step * 4 + 2 * w * kw_step * 4 + 2 * t_dec * kw_step * 2
           + heads_step * t_dec * (w + t_dec) * 4 + 8 * t_dec * (w + t_dec) * 4 + 2 * w * kw_step * 2)
    o = band_attention(
        qkv16_s, pl.BlockSpec((t_dec, qw_step), lambda hg, b: (b, hg)),
        [ck2, qkv16_s], k_specs, [cv2, qkv16_s], v_specs, bias_s, sinks, o, o_shape,
        pl.BlockSpec((t_dec, qw_step), lambda hg, b: (row0 + b, hg)),
        (n_hg, n_seq), kv_heads=kv_per_step, group=group, hd=hd, first_blk=None, block_bytes=blk)
    return o


def _roll_caches(caches, new_rows, t_dec):
    new = jnp.stack(new_rows).reshape(caches.shape[0], caches.shape[1], t_dec, *caches.shape[3:])
    return jnp.concatenate([caches[:, :, t_dec:], new], axis=2)


def _qk_gains(qk_g, n_q_heads, n_k_heads, v_width):
    g = jnp.concatenate([jnp.tile(qk_g[0], n_q_heads), jnp.tile(qk_g[1], n_k_heads), jnp.ones((v_width,), F32)])
    return g.reshape(1, -1).astype(F32)


def kernel(x_prompt, x_sample, cache_a_k, cache_a_v, cache_b_k, cache_b_v, cache_b_logf, cache_c_k, cache_c_v,
           norm_g, ffn_w_in, ffn_w_out, a_w_qkv, a_w_o, a_qk_g, a_rel_bias, b_w_qkvf, b_f_bias, b_w_o, b_qk_g,
           c_w_qkv, c_w_o, c_qk_g, c_sinks, t5_bias):
    batch, seq, d = x_prompt.shape
    n_seq, t_dec, _ = x_sample.shape
    assert batch == 1 and t_dec == CHUNK and seq % 256 == 0
    depth = norm_g.shape[0]
    past_len = cache_b_k.shape[2]
    assert past_len % CHUNK == 0
    n_prompt = batch * seq
    n_samp = n_seq * t_dec
    m = n_prompt + n_samp

    h_a, hd_a = a_rel_bias.shape[2], a_qk_g.shape[2]
    h_b, hd_b = b_f_bias.shape[1], b_qk_g.shape[2]
    h_c, hd_c = c_sinks.shape[1], c_qk_g.shape[2]
    kv_c = cache_c_k.shape[3]
    g_c = h_c // kv_c
    nb_c = cache_c_k.shape[2] // CHUNK
    assert hd_a == V7X_LANES and hd_b == V7X_LANES

    x = jnp.concatenate([x_prompt.reshape(n_prompt, d), x_sample.reshape(n_samp, d)], axis=0)

    ffn_w_in16, ffn_w_out16 = ffn_w_in.astype(BF16), ffn_w_out.astype(BF16)
    a_w_qkv16, a_w_o16 = a_w_qkv.astype(BF16), a_w_o.astype(BF16)
    b_w_qkvf16, b_w_o16 = b_w_qkvf.astype(BF16), b_w_o.astype(BF16)
    c_w_qkv16, c_w_o16 = c_w_qkv.astype(BF16), c_w_o.astype(BF16)

    def half_ffn(x, g, layer, half):
        xn = rms_norm_rows(x, g)
        h = ffn_in(xn, ffn_w_in16, (layer, half))
        return matmul_residual(h, ffn_w_out16, (layer, half), x, FFN_RES, 512)

    a_kp, a_vp, a_ks, a_vs = [], [], [], []
    b_kp, b_vp, b_lp, b_ks, b_vs, b_ls = [], [], [], [], [], []
    c_kp, c_vp, c_ks, c_vs = [], [], [], []

    for i in range(depth):
        kind, j = i % 3, i // 3
        x = half_ffn(x, norm_g[i, 0], i, 0)
        u = rms_norm_rows(x, norm_g[i, 1])
        if kind == 0:
            dq = h_a * hd_a
            gains = _qk_gains(a_qk_g[j], h_a, h_a, dq)
            qkv_p, k_p, v_p = qkv_project(u, 0, n_prompt, a_w_qkv16, (j,), gains, hd_a, dq, dq)
            qkv_s, k_s, v_s = qkv_project(u, n_prompt, n_samp, a_w_qkv16, (j,), gains, hd_a, dq, dq)
            clip = (a_rel_bias.shape[1] - 1) // 2
            o = _band_mixer(qkv_p, qkv_s, m, n_seq, t_dec, cache_a_k[j], cache_a_v[j],
                            functools.partial(_clip_index, clip), a_rel_bias[j], None,
                            d_q=dq, kv_heads_total=h_a, group=1, hd=hd_a, n_prev=NB_A, past_len=past_len,
                            kv_per_step=4, tq=256)
            x = matmul_residual(o, a_w_o16, (j,), x, 1.0, 1024)
            w = min(NB_A * CHUNK, seq)
            a_kp.append(k_p[n_prompt - w:].reshape(batch, w, h_a, hd_a))
            a_vp.append(v_p[n_prompt - w:].reshape(batch, w, h_a, hd_a))
            a_ks.append(k_s)
            a_vs.append(v_s)
        elif kind == 1:
            dq = h_b * hd_b
            gains = _qk_gains(b_qk_g[j], h_b, h_b, dq)
            qkv_p, k_p, v_p = qkv_project(u, 0, n_prompt, b_w_qkvf16, (j,), gains, hd_b, dq, dq)
            qkv_s, k_s, v_s = qkv_project(u, n_prompt, n_samp, b_w_qkvf16, (j,), gains, hd_b, dq, dq)
            w_f = jnp.pad(b_w_qkvf16[j, :, 3 * dq:], ((0, 0), (0, V7X_LANES - h_b)))
            b_f = jnp.pad(b_f_bias[j].astype(F32), (0, V7X_LANES - h_b)).reshape(1, V7X_LANES)
            logf = forget_gate(u, w_f, b_f)
            tq = 1024 if seq % 1024 == 0 else 256
            cum_p = cumsum_time(logf, batch, seq, 256)
            ck = cum_p[0, :, :h_b].T.reshape(h_b, seq // tq, 1, tq)
            o = fox_prompt(qkv_p, cum_p, ck, jax.ShapeDtypeStruct((m, dq), BF16),
                           heads=h_b, hd=hd_b, seq=seq, tq=tq)
            ls = past_len + t_dec
            logf_all = jnp.concatenate(
                [jnp.pad(cache_b_logf[j].astype(F32), ((0, 0), (0, 0), (0, V7X_LANES - h_b))),
                 logf[n_prompt:].reshape(n_seq, t_dec, V7X_LANES)], axis=1)
            cum_s = cumsum_time(logf_all.reshape(n_seq * ls, V7X_LANES), n_seq, ls, CHUNK)
            ck = cum_s[:, :, :h_b].transpose(0, 2, 1)[:, :, None, :]
            o = fox_sample(qkv_s, cache_b_k[j].reshape(n_seq * past_len, dq), cache_b_v[j].reshape(n_seq * past_len, dq),
                           cum_s, ck, o, n_prompt // t_dec,
                           n_seq=n_seq, heads=h_b, hd=hd_b, t_dec=t_dec, past=past_len)
            x = matmul_residual(o, b_w_o16, (j,), x, 1.0, 1024)
            b_kp.append(k_p.reshape(batch, seq, h_b, hd_b))
            b_vp.append(v_p.reshape(batch, seq, h_b, hd_b))
            b_lp.append(logf[:n_prompt, :h_b].reshape(batch, seq, h_b))
            b_ks.append(k_s.reshape(n_seq, t_dec, h_b, hd_b))
            b_vs.append(v_s.reshape(n_seq, t_dec, h_b, hd_b))
            b_ls.append(logf[n_prompt:, :h_b].reshape(n_seq, t_dec, h_b))
        else:
            dq, kvw = h_c * hd_c, kv_c * hd_c
            gains = _qk_gains(c_qk_g[j], h_c, kv_c, kvw)
            qkv_p, k_p, v_p = qkv_project(u, 0, n_prompt, c_w_qkv16, (j,), gains, hd_c, dq, kvw)
            qkv_s, k_s, v_s = qkv_project(u, n_prompt, n_samp, c_w_qkv16, (j,), gains, hd_c, dq, kvw)
            o = _band_mixer(qkv_p, qkv_s, m, n_seq, t_dec, cache_c_k[j], cache_c_v[j],
                            functools.partial(_t5_bucket, t5_bias.shape[0]), t5_bias, c_sinks[j].astype(F32),
                            d_q=dq, kv_heads_total=kv_c, group=g_c, hd=hd_c, n_prev=nb_c, past_len=past_len,
                            kv_per_step=2, tq=256)
            x = matmul_residual(o, c_w_o16, (j,), x, 1.0, 1024)
            w = min(nb_c * CHUNK, seq)
            c_kp.append(k_p[n_prompt - w:].reshape(batch, w, kv_c, hd_c))
            c_vp.append(v_p[n_prompt - w:].reshape(batch, w, kv_c, hd_c))
            c_ks.append(k_s)
            c_vs.append(v_s)
        x = half_ffn(x, norm_g[i, 2], i, 1)

    y_prompt = x[:n_prompt].reshape(batch, seq, d)
    y_sample = x[n_prompt:].reshape(n_seq, t_dec, d)
    return (y_prompt, y_sample,
            jnp.stack(a_kp), jnp.stack(a_vp),
            _roll_caches(cache_a_k, a_ks, t_dec), _roll_caches(cache_a_v, a_vs, t_dec),
            jnp.stack(b_kp), jnp.stack(b_vp), jnp.stack(b_lp), jnp.stack(b_ks), jnp.stack(b_vs), jnp.stack(b_ls),
            jnp.stack(c_kp), jnp.stack(c_vp),
            _roll_caches(cache_c_k, c_ks, t_dec), _roll_caches(cache_c_v, c_vs, t_dec))
```

```python
import functools
import math

import jax
import jax.numpy as jnp
from jax import lax
from jax.experimental import pallas as pl
from jax.experimental.pallas import tpu as pltpu

CHUNK = 64
NB_A = 8
EPS = 1e-6
NEG_INF = -1e30
FFN_RES = 0.5
T5_MAX_DIST = 128
LOG2E = math.log2(math.e)

V7X_LANES = 128
V7X_SUBLANES = 8
V7X_VMEM_BYTES = 64 * 1024 * 1024
VMEM_HEADROOM_BYTES = 14 * 1024 * 1024

F32 = jnp.float32
BF16 = jnp.bfloat16


def _params(semantics, block_bytes):
    limit = min(2 * block_bytes + VMEM_HEADROOM_BYTES, V7X_VMEM_BYTES - 2 * 1024 * 1024)
    return pltpu.CompilerParams(dimension_semantics=semantics, vmem_limit_bytes=int(limit))


def _row_tile(m, want):
    t = want
    while m % t:
        t //= 2
    return t


def _dot(a, b):
    return jnp.dot(a, b, preferred_element_type=F32)


def _dot_nt(a, b):
    return lax.dot_general(a, b, (((1,), (1,)), ((), ())), preferred_element_type=F32)


def _rms_kernel(x_ref, g_ref, o_ref):
    x = x_ref[...]
    ms = jnp.mean(x * x, axis=-1, keepdims=True)
    o_ref[...] = ((x * lax.rsqrt(ms + EPS)) * g_ref[...]).astype(o_ref.dtype)


def rms_norm_rows(x, g):
    m, d = x.shape
    tm = _row_tile(m, 256)
    return pl.pallas_call(
        _rms_kernel,
        grid=(m // tm,),
        in_specs=[pl.BlockSpec((tm, d), lambda i: (i, 0)),
                  pl.BlockSpec((1, d), lambda i: (0, 0))],
        out_specs=pl.BlockSpec((tm, d), lambda i: (i, 0)),
        out_shape=jax.ShapeDtypeStruct((m, d), BF16),
        compiler_params=_params(("arbitrary",), tm * d * 6),
        name="rms_norm",
    )(x, g.reshape(1, d).astype(F32))


def _ffn_in_kernel(x_ref, wg_ref, wu_ref, o_ref):
    x = x_ref[...]
    gate = _dot(x, wg_ref[...])
    up = _dot(x, wu_ref[...])
    o_ref[...] = (gate * jax.nn.sigmoid(gate) * up).astype(o_ref.dtype)


def _ffn_tile(f):
    for tf in (256, 128):
        if f % tf == 0:
            return tf
    raise ValueError(f"D_FF={f} must be a multiple of {V7X_LANES}")


def _weight_spec(lead, k, tn, col):
    return pl.BlockSpec((None,) * len(lead) + (k, tn), lambda i, j: (*lead, 0, col(j)))


def ffn_in(xn, w_in16, lead):
    m, k = xn.shape
    f = w_in16.shape[-1] // 2
    tf = _ffn_tile(f)
    n_f = f // tf
    tm = _row_tile(m, 1024)
    blk = tm * k * 2 + 2 * k * tf * 2 + tm * tf * 2 + tm * 2 * tf * 4
    return pl.pallas_call(
        _ffn_in_kernel,
        grid=(m // tm, n_f),
        in_specs=[pl.BlockSpec((tm, k), lambda i, j: (i, 0)),
                  _weight_spec(lead, k, tf, lambda j: j),
                  _weight_spec(lead, k, tf, lambda j: n_f + j)],
        out_specs=pl.BlockSpec((tm, tf), lambda i, j: (i, j)),
        out_shape=jax.ShapeDtypeStruct((m, f), BF16),
        compiler_params=_params(("arbitrary", "arbitrary"), blk),
        name="ffn_in",
    )(xn, w_in16, w_in16)


def _mm_res_kernel(a_ref, w_ref, r_ref, o_ref, *, scale):
    acc = _dot(a_ref[...], w_ref[...])
    o_ref[...] = r_ref[...] + scale * acc


def matmul_residual(a, w, lead, res, scale, tm_want, row0=0, rows=None):
    m, k = a.shape
    n = w.shape[-1]
    rows = m if rows is None else rows
    tm = _row_tile(math.gcd(row0, rows) if row0 else rows, tm_want)
    tn = _row_tile(n, 512)
    i0 = row0 // tm
    blk = tm * k * 2 + k * tn * 2 + 3 * tm * tn * 4
    return pl.pallas_call(
        functools.partial(_mm_res_kernel, scale=scale),
        grid=(rows // tm, n // tn),
        in_specs=[pl.BlockSpec((tm, k), lambda i, j: (i0 + i, 0)),
                  _weight_spec(lead, k, tn, lambda j: j),
                  pl.BlockSpec((tm, tn), lambda i, j: (i0 + i, j))],
        out_specs=pl.BlockSpec((tm, tn), lambda i, j: (i, j)),
        out_shape=jax.ShapeDtypeStruct((rows, n), F32),
        compiler_params=_params(("arbitrary", "arbitrary"), blk),
        name="matmul_residual",
    )(a, w, res)


def _head_rms(acc, hd, normalise):
    tn = acc.shape[1]
    parts = []
    if hd == V7X_LANES:
        for h in range(tn // hd):
            xh = acc[:, h * hd:(h + 1) * hd]
            ms = jnp.mean(xh * xh, axis=-1, keepdims=True)
            parts.append(xh * jnp.where(normalise, lax.rsqrt(ms + EPS), 1.0))
    else:
        assert hd * 2 == V7X_LANES
        lo = lax.broadcasted_iota(jnp.int32, (1, V7X_LANES), 1) < hd
        for h in range(tn // V7X_LANES):
            xh = acc[:, h * V7X_LANES:(h + 1) * V7X_LANES]
            x2 = xh * xh
            s_lo = jnp.sum(jnp.where(lo, x2, 0.0), axis=-1, keepdims=True)
            s_hi = jnp.sum(jnp.where(lo, 0.0, x2), axis=-1, keepdims=True)
            ms = jnp.where(lo, s_lo, s_hi) * (1.0 / hd)
            parts.append(xh * jnp.where(normalise, lax.rsqrt(ms + EPS), 1.0))
    return jnp.concatenate(parts, axis=-1)


def _qkv_kernel(x_ref, w_ref, g_ref, o16_ref, o32_ref, *, hd, n_norm_tiles, row_chunks):
    normalise = pl.program_id(1) < n_norm_tiles
    g = g_ref[...]
    rc = x_ref.shape[0] // row_chunks
    for c in range(row_chunks):
        rows = slice(c * rc, (c + 1) * rc)
        y = _head_rms(_dot(x_ref[rows, :], w_ref[...]), hd, normalise) * g
        o16_ref[rows, :] = y.astype(o16_ref.dtype)
        o32_ref[rows, :] = y


def qkv_project(xn, row0, rows, w, lead, gains, hd, q_width, kv_width):
    k = xn.shape[1]
    n = q_width + 2 * kv_width
    tm = _row_tile(math.gcd(row0, rows) if row0 else rows, 1024)
    tn = _row_tile(math.gcd(q_width, kv_width), 512)
    n_q, n_k = q_width // tn, kv_width // tn
    i0 = row0 // tm
    row_chunks = 4 if tm % 64 == 0 else 1
    blk = tm * k * 2 + k * tn * 2 + tm * tn * (2 + 4 + 4)
    return pl.pallas_call(
        functools.partial(_qkv_kernel, hd=hd, n_norm_tiles=n_q + n_k, row_chunks=row_chunks),
        grid=(rows // tm, n // tn),
        in_specs=[pl.BlockSpec((tm, k), lambda i, j: (i0 + i, 0)),
                  _weight_spec(lead, k, tn, lambda j: j),
                  pl.BlockSpec((1, tn), lambda i, j: (0, j))],
        out_specs=[pl.BlockSpec((tm, tn), lambda i, j: (i, j)),
                   pl.BlockSpec((tm, tn), lambda i, j: (i, jnp.maximum(j - n_q, 0)))],
        out_shape=[jax.ShapeDtypeStruct((rows, n), BF16),
                   jax.ShapeDtypeStruct((rows, 2 * kv_width), F32)],
        compiler_params=_params(("arbitrary", "arbitrary"), blk),
        name="qkv_project",
    )(xn, w, gains)


def _fgate_kernel(x_ref, w_ref, b_ref, o_ref):
    z = _dot(x_ref[...], w_ref[...]) + b_ref[...]
    o_ref[...] = jnp.minimum(z, 0.0) - jnp.log1p(jnp.exp(-jnp.abs(z)))


def forget_gate(xn, w_pad, b_pad):
    m, k = xn.shape
    n = w_pad.shape[1]
    tm = _row_tile(m, 1024)
    blk = tm * k * 2 + k * n * 2 + tm * n * 8
    return pl.pallas_call(
        _fgate_kernel,
        grid=(m // tm,),
        in_specs=[pl.BlockSpec((tm, k), lambda i: (i, 0)),
                  pl.BlockSpec((k, n), lambda i: (0, 0)),
                  pl.BlockSpec((1, n), lambda i: (0, 0))],
        out_specs=pl.BlockSpec((tm, n), lambda i: (i, 0)),
        out_shape=jax.ShapeDtypeStruct((m, n), F32),
        compiler_params=_params(("arbitrary",), blk),
        name="forget_gate",
    )(xn, w_pad, b_pad)


def _cumsum_kernel(x_ref, o_ref, *, blk):
    length, width = x_ref.shape
    row = lax.broadcasted_iota(jnp.int32, (blk, blk), 0)
    col = lax.broadcasted_iota(jnp.int32, (blk, blk), 1)
    tri = (row >= col).astype(F32)

    def body(i, carry):
        start = pl.multiple_of(i * blk, blk)
        xs = x_ref[pl.ds(start, blk), :]
        c = jnp.dot(tri, xs, precision=lax.Precision.HIGHEST, preferred_element_type=F32) + carry
        o_ref[pl.ds(start, blk), :] = c
        return c[blk - 1:blk, :]

    lax.fori_loop(0, length // blk, body, jnp.zeros((1, width), F32))


def cumsum_time(x, n_seq, length, blk):
    width = x.shape[1]
    return pl.pallas_call(
        functools.partial(_cumsum_kernel, blk=blk),
        grid=(n_seq,),
        in_specs=[pl.BlockSpec((length, width), lambda i: (i, 0))],
        out_specs=pl.BlockSpec((None, length, width), lambda i: (i, 0, 0)),
        out_shape=jax.ShapeDtypeStruct((n_seq, length, width), F32),
        compiler_params=_params(("arbitrary",), 2 * length * width * 4),
        name="cumsum_time",
    )(x)


def _head_rows(ref, head, hd):
    if len(ref.shape) == 3:
        return ref[:, head, :]
    return ref[:, head * hd:(head + 1) * hd]


def _band_kernel(*refs, n_kblk, kv_heads, group, hd, scale2, has_sink, first_blk, heads_per_step):
    q_ref = refs[0]
    k_refs = refs[1:1 + n_kblk]
    v_refs = refs[1 + n_kblk:1 + 2 * n_kblk]
    bias_ref = refs[1 + 2 * n_kblk]
    sink_ref = refs[2 + 2 * n_kblk] if has_sink else None
    o_ref = refs[-1]

    hg = pl.program_id(0)
    qi = pl.program_id(1)
    widths = [r.shape[0] for r in k_refs]
    logits = []
    for kv in range(kv_heads):
        kparts = [_head_rows(r, kv, hd).astype(BF16) for r in k_refs]
        for g in range(group):
            h = kv * group + g
            qh = q_ref[:, h * hd:(h + 1) * hd]
            sparts = []
            for t, kp in enumerate(kparts):
                st = _dot_nt(qh, kp)
                if first_blk is not None:
                    st = jnp.where(first_blk(qi, t) >= 0, st, NEG_INF)
                sparts.append(st)
            s = jnp.concatenate(sparts, axis=1) if len(sparts) > 1 else sparts[0]
            logits.append(s * scale2 + bias_ref[h])
    outs = []
    for kv in range(kv_heads):
        vparts = [_head_rows(r, kv, hd).astype(BF16) for r in v_refs]
        for g in range(group):
            h = kv * group + g
            z = logits[h]
            m = jnp.max(z, axis=-1, keepdims=True)
            if has_sink:
                sk = sink_ref[hg * heads_per_step + h] * LOG2E
                m = jnp.maximum(m, sk)
            e = jnp.exp2(z - m)
            denom = jnp.sum(e, axis=-1, keepdims=True)
            if has_sink:
                denom = denom + jnp.exp2(sk - m)
            p = (e / denom).astype(BF16)
            off = 0
            acc = None
            for w_t, vp in zip(widths, vparts):
                part = _dot(p[:, off:off + w_t], vp)
                acc = part if acc is None else acc + part
                off += w_t
            outs.append(acc)
    per_store = V7X_LANES // hd
    for c in range(len(outs) // per_store):
        piece = outs[c * per_store:(c + 1) * per_store]
        val = jnp.concatenate(piece, axis=1) if per_store > 1 else piece[0]
        o_ref[:, c * V7X_LANES:(c + 1) * V7X_LANES] = val.astype(o_ref.dtype)


def _drop_alias_ref(kern):
    def body(*refs):
        return kern(*refs[:-2], refs[-1])
    return body


def band_attention(q_arr, q_spec, k_arrs, k_specs, v_arrs, v_specs, bias, sinks, o_prev, o_shape, o_spec,
                   grid, *, kv_heads, group, hd, first_blk, block_bytes):
    heads_per_step = kv_heads * group
    n_kblk = len(k_arrs)
    has_sink = sinks is not None
    tq, nk = bias.shape[1], bias.shape[2]
    in_specs = [q_spec, *k_specs, *v_specs, pl.BlockSpec((heads_per_step, tq, nk), lambda a, b: (a, 0, 0))]
    args = [q_arr, *k_arrs, *v_arrs, bias]
    if has_sink:
        in_specs.append(pl.BlockSpec(memory_space=pltpu.SMEM))
        args.append(sinks)
    body = functools.partial(_band_kernel, n_kblk=n_kblk, kv_heads=kv_heads, group=group, hd=hd,
                             scale2=hd ** -0.5 * LOG2E, has_sink=has_sink, first_blk=first_blk,
                             heads_per_step=heads_per_step)
    aliases = {}
    if o_prev is not None:
        in_specs.append(pl.BlockSpec(memory_space=pl.ANY))
        args.append(o_prev)
        aliases = {len(args) - 1: 0}
        body = _drop_alias_ref(body)
    return pl.pallas_call(
        body,
        grid=grid,
        in_specs=in_specs,
        out_specs=o_spec,
        out_shape=o_shape,
        input_output_aliases=aliases,
        compiler_params=_params(("arbitrary", "arbitrary"), block_bytes),
        name="band_attention",
    )(*args)


def _head_column(cum_blk, h):
    lane = lax.broadcasted_iota(jnp.int32, (1, V7X_LANES), 1)
    return jnp.sum(jnp.where(lane == h, cum_blk, 0.0), axis=-1, keepdims=True)


def _fox_prompt_kernel(q_ref, k_ref, v_ref, cum_ref, ck_ref, o_ref, *, tq, scale2):
    h = pl.program_id(0)
    qb = pl.program_id(1)
    q = q_ref[...]
    hd = q.shape[1]
    cq = _head_column(cum_ref[...], h) * LOG2E

    def step(kb, carry, diagonal):
        m, l, acc = carry
        start = pl.multiple_of(kb * tq, tq)
        kblk = k_ref[pl.ds(start, tq), :]
        vblk = v_ref[pl.ds(start, tq), :]
        ck = ck_ref[kb] * LOG2E
        z = _dot_nt(q, kblk) * scale2 + (cq - ck)
        if diagonal:
            row = lax.broadcasted_iota(jnp.int32, (tq, tq), 0)
            col = lax.broadcasted_iota(jnp.int32, (tq, tq), 1)
            z = jnp.where(col <= row, z, NEG_INF)
        m_new = jnp.maximum(m, jnp.max(z, axis=-1, keepdims=True))
        alpha = jnp.exp2(m - m_new)
        p = jnp.exp2(z - m_new)
        l_new = alpha * l + jnp.sum(p, axis=-1, keepdims=True)
        acc_new = alpha * acc + _dot(p.astype(BF16), vblk)
        return m_new, l_new, acc_new

    carry = (jnp.full((tq, 1), NEG_INF, F32), jnp.zeros((tq, 1), F32), jnp.zeros((tq, hd), F32))
    carry = lax.fori_loop(0, qb // 2, lambda i, c: step(2 * i + 1, step(2 * i, c, False), False), carry)
    carry = lax.fori_loop((qb // 2) * 2, qb, lambda kb, c: step(kb, c, False), carry)
    _, l, acc = step(qb, carry, True)
    o_ref[...] = (acc / l).astype(o_ref.dtype)


def fox_prompt(qkv16, cum, ck, o_shape, *, heads, hd, seq, tq):
    blk = 2 * seq * hd * 2 + 2 * tq * hd * 2 + tq * V7X_LANES * 4 + 8 * seq * 4
    return pl.pallas_call(
        functools.partial(_fox_prompt_kernel, tq=tq, scale2=hd ** -0.5 * LOG2E),
        grid=(heads, seq // tq),
        in_specs=[pl.BlockSpec((tq, hd), lambda h, qb: (qb, h)),
                  pl.BlockSpec((seq, hd), lambda h, qb: (0, heads + h)),
                  pl.BlockSpec((seq, hd), lambda h, qb: (0, 2 * heads + h)),
                  pl.BlockSpec((None, tq, V7X_LANES), lambda h, qb: (0, qb, 0)),
                  pl.BlockSpec((None, seq // tq, 1, tq), lambda h, qb: (h, 0, 0, 0))],
        out_specs=pl.BlockSpec((tq, hd), lambda h, qb: (qb, h)),
        out_shape=o_shape,
        compiler_params=pltpu.CompilerParams(dimension_semantics=("arbitrary", "arbitrary"),
                                             vmem_limit_bytes=int(min(2 * blk + 32 * 1024 * 1024,
                                                                      V7X_VMEM_BYTES - 2 * 1024 * 1024))),
        name="fox_prompt",
    )(qkv16, qkv16, qkv16, cum, ck)


def _fox_sample_kernel(q_ref, kc_ref, vc_ref, kn_ref, vn_ref, cum_ref, ck_ref, prev_ref, o_ref, *,
                       heads_per_step, hd, past, scale2):
    del prev_ref
    hg = pl.program_id(0)
    t = q_ref.shape[0]
    row = lax.broadcasted_iota(jnp.int32, (t, past + t), 0)
    col = lax.broadcasted_iota(jnp.int32, (t, past + t), 1)
    causal = col <= row + past
    logits = []
    for hh in range(heads_per_step):
        sl = slice(hh * hd, (hh + 1) * hd)
        qh = q_ref[:, sl]
        cq = _head_column(cum_ref[...], hg * heads_per_step + hh) * LOG2E
        ck = ck_ref[hh] * LOG2E
        s = jnp.concatenate([_dot_nt(qh, kc_ref[:, hh, :].astype(BF16)), _dot_nt(qh, kn_ref[:, sl])], axis=1)
        logits.append(jnp.where(causal, s * scale2 + (cq - ck), NEG_INF))
    for hh in range(heads_per_step):
        sl = slice(hh * hd, (hh + 1) * hd)
        z = logits[hh]
        m = jnp.max(z, axis=-1, keepdims=True)
        e = jnp.exp2(z - m)
        p = (e / jnp.sum(e, axis=-1, keepdims=True)).astype(BF16)
        o = _dot(p[:, :past], vc_ref[:, hh, :].astype(BF16)) + _dot(p[:, past:], vn_ref[:, sl])
        o_ref[:, sl] = o.astype(o_ref.dtype)


def fox_sample(qkv16_s, cache_k, cache_v, cum, ck, o_prev, row0_blocks, *, n_seq, heads, hd, t_dec, past):
    hps = V7X_SUBLANES
    assert heads % hps == 0
    width = hps * hd
    n_hg = heads // hps
    ls = past + t_dec
    blk = (2 * past * width * 4 + 3 * t_dec * width * 2 + t_dec * V7X_LANES * 4 + hps * 8 * ls * 4
           + 2 * past * width * 2 + 6 * t_dec * ls * 4)
    return pl.pallas_call(
        functools.partial(_fox_sample_kernel, heads_per_step=hps, hd=hd, past=past, scale2=hd ** -0.5 * LOG2E),
        grid=(n_hg, n_seq),
        in_specs=[pl.BlockSpec((t_dec, width), lambda hg, b: (b, hg)),
                  pl.BlockSpec((past, hps, hd), lambda hg, b: (b, hg, 0)),
                  pl.BlockSpec((past, hps, hd), lambda hg, b: (b, hg, 0)),
                  pl.BlockSpec((t_dec, width), lambda hg, b: (b, n_hg + hg)),
                  pl.BlockSpec((t_dec, width), lambda hg, b: (b, 2 * n_hg + hg)),
                  pl.BlockSpec((None, t_dec, V7X_LANES), lambda hg, b: (b, past // t_dec, 0)),
                  pl.BlockSpec((None, hps, 1, ls), lambda hg, b: (b, hg, 0, 0)),
                  pl.BlockSpec(memory_space=pl.ANY)],
        out_specs=pl.BlockSpec((t_dec, width), lambda hg, b: (row0_blocks + b, hg)),
        out_shape=jax.ShapeDtypeStruct(o_prev.shape, o_prev.dtype),
        input_output_aliases={7: 0},
        compiler_params=_params(("arbitrary", "arbitrary"), blk),
        name="fox_sample",
    )(qkv16_s, cache_k, cache_v, qkv16_s, qkv16_s, cum, ck, o_prev)


def _toeplitz(vals, tq, nk):
    w = tq + nk
    flat = jnp.tile(vals, (1, tq))[:, :tq * w]
    return flat.reshape(vals.shape[0], tq, w)[:, :, :nk]


def _band_bias(rel_to_index, table, q0, k0, tq, nk, n_prev):
    d = jnp.arange(tq + nk + 1)
    j_minus_i = jnp.where(d < nk, d, d - (tq + nk + 1))
    vals = table.astype(F32)[rel_to_index((q0 - k0) - j_minus_i)].T * LOG2E
    qc = (q0 + jnp.arange(tq))[:, None] // CHUNK
    kc = (k0 + jnp.arange(nk))[None, :] // CHUNK
    valid = (kc <= qc) & (kc >= qc - n_prev)
    return jnp.where(valid[None], _toeplitz(vals, tq, nk), NEG_INF)


def _clip_index(clip, rel):
    return jnp.clip(rel, -clip, clip) + clip


def _t5_bucket(n_buckets, rel):
    half = n_buckets // 2
    max_exact = half // 2
    n = jnp.abs(rel)
    ratio = jnp.maximum(n, max_exact).astype(F32) / max_exact
    large = max_exact + (jnp.log(ratio) / math.log(T5_MAX_DIST / max_exact) * (half - max_exact)).astype(jnp.int32)
    large = jnp.minimum(large, half - 1)
    return jnp.where(rel < 0, half, 0) + jnp.where(n < max_exact, n, large)


def _band_mixer(qkv16_p, qkv16_s, m_total, n_seq, t_dec, cache_k, cache_v, rel_to_index, table, sinks, *,
                d_q, kv_heads_total, group, hd, n_prev, past_len, kv_per_step, tq):
    n_prompt = qkv16_p.shape[0]
    kvw = kv_heads_total * hd
    qw_step = kv_per_step * group * hd
    kw_step = kv_per_step * hd
    n_hg = kv_heads_total // kv_per_step
    heads_step = kv_per_step * group
    win = n_prev * CHUNK
    tkb = math.gcd(tq, win)
    n_kblk = (tq + win) // tkb
    n_back = win // tkb
    ratio = tq // tkb
    k_col0 = d_q // kw_step
    v_col0 = (d_q + kvw) // kw_step
    o_shape = jax.ShapeDtypeStruct((m_total, d_q), BF16)

    bias_p = _band_bias(rel_to_index, table, win, 0, tq, tq + win, n_prev)

    def first_blk(qi, t):
        return qi * ratio - n_back + t

    def kmap(t, col0):
        return lambda hg, qi: (jnp.maximum(qi * ratio - n_back + t, 0), col0 + hg)

    k_specs = [pl.BlockSpec((tkb, kw_step), kmap(t, k_col0)) for t in range(n_kblk)]
    v_specs = [pl.BlockSpec((tkb, kw_step), kmap(t, v_col0)) for t in range(n_kblk)]
    blk = (tq * qw_step * 2 * 2 + 2 * n_kblk * tkb * kw_step * 2 + heads_step * tq * (tq + win) * 4
           + 8 * tq * (tq + win) * 4)
    o = band_attention(
        qkv16_p, pl.BlockSpec((tq, qw_step), lambda hg, qi: (qi, hg)),
        [qkv16_p] * n_kblk, k_specs, [qkv16_p] * n_kblk, v_specs, bias_p, sinks, None, o_shape,
        pl.BlockSpec((tq, qw_step), lambda hg, qi: (qi, hg)),
        (n_hg, n_prompt // tq), kv_heads=kv_per_step, group=group, hd=hd, first_blk=first_blk, block_bytes=blk)

    w = cache_k.shape[1]
    bias_s = _band_bias(rel_to_index, table, past_len, past_len - w, t_dec, w + t_dec, n_prev)
    row0 = n_prompt // t_dec
    if hd == V7X_LANES and kv_heads_total % V7X_SUBLANES == 0:
        kv_per_step = V7X_SUBLANES
        qw_step, kw_step, heads_step = kv_per_step * group * hd, kv_per_step * hd, kv_per_step * group
        n_hg, k_col0, v_col0 = kv_heads_total // kv_per_step, d_q // kw_step, (d_q + kvw) // kw_step
        ck2 = cache_k.reshape(n_seq * w, kv_heads_total, hd)
        cv2 = cache_v.reshape(n_seq * w, kv_heads_total, hd)
        cache_spec = pl.BlockSpec((w, kv_per_step, hd), lambda hg, b: (b, hg, 0))
    else:
        ck2 = cache_k.reshape(n_seq * w, kvw)
        cv2 = cache_v.reshape(n_seq * w, kvw)
        cache_spec = pl.BlockSpec((w, kw_step), lambda hg, b: (b, hg))
    k_specs = [cache_spec, pl.BlockSpec((t_dec, kw_step), lambda hg, b: (b, k_col0 + hg))]
    v_specs = [cache_spec, pl.BlockSpec((t_dec, kw_step), lambda hg, b: (b, v_col0 + hg))]
    blk = (t_dec * qw_step * 4 + 2 * w * kw_step * 4 + 2 * t_dec * kw_step * 2
           + heads_step * t_dec * (w + t_dec) * 4 + 8 * t_dec * (w + t_dec) * 4 + 2 * w * kw_step * 2)
    o = band_attention(
        qkv16_s, pl.BlockSpec((t_dec, qw_step), lambda hg, b: (b, hg)),
        [ck2, qkv16_s], k_specs, [cv2, qkv16_s], v_specs, bias_s, sinks, o, o_shape,
        pl.BlockSpec((t_dec, qw_step), lambda hg, b: (row0 + b, hg)),
        (n_hg, n_seq), kv_heads=kv_per_step, group=group, hd=hd, first_blk=None, block_bytes=blk)
    return o


def _roll_caches(caches, new_rows, t_dec):
    new = jnp.stack(new_rows).reshape(caches.shape[0], caches.shape[1], t_dec, *caches.shape[3:])
    return jnp.concatenate([caches[:, :, t_dec:], new], axis=2)


def _qk_gains(qk_g, n_q_heads, n_k_heads, v_width):
    g = jnp.concatenate([jnp.tile(qk_g[0], n_q_heads), jnp.tile(qk_g[1], n_k_heads), jnp.ones((v_width,), F32)])
    return g.reshape(1, -1).astype(F32)


def kernel(x_prompt, x_sample, cache_a_k, cache_a_v, cache_b_k, cache_b_v, cache_b_logf, cache_c_k, cache_c_v,
           norm_g, ffn_w_in, ffn_w_out, a_w_qkv, a_w_o, a_qk_g, a_rel_bias, b_w_qkvf, b_f_bias, b_w_o, b_qk_g,
           c_w_qkv, c_w_o, c_qk_g, c_sinks, t5_bias):
    batch, seq, d = x_prompt.shape
    n_seq, t_dec, _ = x_sample.shape
    assert batch == 1 and t_dec == CHUNK and seq % 256 == 0
    depth = norm_g.shape[0]
    past_len = cache_b_k.shape[2]
    assert past_len % CHUNK == 0
    n_prompt = batch * seq
    n_samp = n_seq * t_dec
    m = n_prompt + n_samp

    h_a, hd_a = a_rel_bias.shape[2], a_qk_g.shape[2]
    h_b, hd_b = b_f_bias.shape[1], b_qk_g.shape[2]
    h_c, hd_c = c_sinks.shape[1], c_qk_g.shape[2]
    kv_c = cache_c_k.shape[3]
    g_c = h_c // kv_c
    nb_c = cache_c_k.shape[2] // CHUNK
    assert hd_a == V7X_LANES and hd_b == V7X_LANES

    x = jnp.concatenate([x_prompt.reshape(n_prompt, d), x_sample.reshape(n_samp, d)], axis=0)

    ffn_w_in16, ffn_w_out16 = ffn_w_in.astype(BF16), ffn_w_out.astype(BF16)
    a_w_qkv16, a_w_o16 = a_w_qkv.astype(BF16), a_w_o.astype(BF16)
    b_w_qkvf16, b_w_o16 = b_w_qkvf.astype(BF16), b_w_o.astype(BF16)
    c_w_qkv16, c_w_o16 = c_w_qkv.astype(BF16), c_w_o.astype(BF16)

    def half_ffn(x, g, layer, half, split_rows=False):
        xn = rms_norm_rows(x, g)
        h = ffn_in(xn, ffn_w_in16, (layer, half))
        if not split_rows:
            return matmul_residual(h, ffn_w_out16, (layer, half), x, FFN_RES, 512)
        return (matmul_residual(h, ffn_w_out16, (layer, half), x, FFN_RES, 512, 0, n_prompt),
                matmul_residual(h, ffn_w_out16, (layer, half), x, FFN_RES, 512, n_prompt, n_samp))

    a_kp, a_vp, a_ks, a_vs = [], [], [], []
    b_kp, b_vp, b_lp, b_ks, b_vs, b_ls = [], [], [], [], [], []
    c_kp, c_vp, c_ks, c_vs = [], [], [], []

    for i in range(depth):
        kind, j = i % 3, i // 3
        x = half_ffn(x, norm_g[i, 0], i, 0)
        u = rms_norm_rows(x, norm_g[i, 1])
        if kind == 0:
            dq = h_a * hd_a
            gains = _qk_gains(a_qk_g[j], h_a, h_a, dq)
            qkv_p, kv_p = qkv_project(u, 0, n_prompt, a_w_qkv16, (j,), gains, hd_a, dq, dq)
            qkv_s, kv_s = qkv_project(u, n_prompt, n_samp, a_w_qkv16, (j,), gains, hd_a, dq, dq)
            k_p, v_p, k_s, v_s = kv_p[:, :dq], kv_p[:, dq:], kv_s[:, :dq], kv_s[:, dq:]
            clip = (a_rel_bias.shape[1] - 1) // 2
            o = _band_mixer(qkv_p, qkv_s, m, n_seq, t_dec, cache_a_k[j], cache_a_v[j],
                            functools.partial(_clip_index, clip), a_rel_bias[j], None,
                            d_q=dq, kv_heads_total=h_a, group=1, hd=hd_a, n_prev=NB_A, past_len=past_len,
                            kv_per_step=4, tq=256)
            x = matmul_residual(o, a_w_o16, (j,), x, 1.0, 1024)
            w = min(NB_A * CHUNK, seq)
            a_kp.append(k_p[n_prompt - w:].reshape(batch, w, h_a, hd_a))
            a_vp.append(v_p[n_prompt - w:].reshape(batch, w, h_a, hd_a))
            a_ks.append(k_s)
            a_vs.append(v_s)
        elif kind == 1:
            dq = h_b * hd_b
            gains = _qk_gains(b_qk_g[j], h_b, h_b, dq)
            qkv_p, kv_p = qkv_project(u, 0, n_prompt, b_w_qkvf16, (j,), gains, hd_b, dq, dq)
            qkv_s, kv_s = qkv_project(u, n_prompt, n_samp, b_w_qkvf16, (j,), gains, hd_b, dq, dq)
            k_p, v_p, k_s, v_s = kv_p[:, :dq], kv_p[:, dq:], kv_s[:, :dq], kv_s[:, dq:]
            w_f = jnp.pad(b_w_qkvf16[j, :, 3 * dq:], ((0, 0), (0, V7X_LANES - h_b)))
            b_f = jnp.pad(b_f_bias[j].astype(F32), (0, V7X_LANES - h_b)).reshape(1, V7X_LANES)
            logf = forget_gate(u, w_f, b_f)
            tq = 1024 if seq % 1024 == 0 else 256
            cum_p = cumsum_time(logf, batch, seq, 256)
            ck = cum_p[0, :, :h_b].T.reshape(h_b, seq // tq, 1, tq)
            o = fox_prompt(qkv_p, cum_p, ck, jax.ShapeDtypeStruct((m, dq), BF16),
                           heads=h_b, hd=hd_b, seq=seq, tq=tq)
            ls = past_len + t_dec
            logf_all = jnp.concatenate(
                [jnp.pad(cache_b_logf[j].astype(F32), ((0, 0), (0, 0), (0, V7X_LANES - h_b))),
                 logf[n_prompt:].reshape(n_seq, t_dec, V7X_LANES)], axis=1)
            cum_s = cumsum_time(logf_all.reshape(n_seq * ls, V7X_LANES), n_seq, ls, CHUNK)
            ck = cum_s[:, :, :h_b].transpose(0, 2, 1)[:, :, None, :]
            o = fox_sample(qkv_s, cache_b_k[j].reshape(n_seq * past_len, h_b, hd_b),
                           cache_b_v[j].reshape(n_seq * past_len, h_b, hd_b), cum_s, ck, o, n_prompt // t_dec,
                           n_seq=n_seq, heads=h_b, hd=hd_b, t_dec=t_dec, past=past_len)
            x = matmul_residual(o, b_w_o16, (j,), x, 1.0, 1024)
            b_kp.append(k_p.reshape(batch, seq, h_b, hd_b))
            b_vp.append(v_p.reshape(batch, seq, h_b, hd_b))
            b_lp.append(logf[:n_prompt, :h_b].reshape(batch, seq, h_b))
            b_ks.append(k_s.reshape(n_seq, t_dec, h_b, hd_b))
            b_vs.append(v_s.reshape(n_seq, t_dec, h_b, hd_b))
            b_ls.append(logf[n_prompt:, :h_b].reshape(n_seq, t_dec, h_b))
        else:
            dq, kvw = h_c * hd_c, kv_c * hd_c
            gains = _qk_gains(c_qk_g[j], h_c, kv_c, kvw)
            qkv_p, kv_p = qkv_project(u, 0, n_prompt, c_w_qkv16, (j,), gains, hd_c, dq, kvw)
            qkv_s, kv_s = qkv_project(u, n_prompt, n_samp, c_w_qkv16, (j,), gains, hd_c, dq, kvw)
            k_p, v_p, k_s, v_s = kv_p[:, :kvw], kv_p[:, kvw:], kv_s[:, :kvw], kv_s[:, kvw:]
            o = _band_mixer(qkv_p, qkv_s, m, n_seq, t_dec, cache_c_k[j], cache_c_v[j],
                            functools.partial(_t5_bucket, t5_bias.shape[0]), t5_bias, c_sinks[j].astype(F32),
                            d_q=dq, kv_heads_total=kv_c, group=g_c, hd=hd_c, n_prev=nb_c, past_len=past_len,
                            kv_per_step=2, tq=256)
            x = matmul_residual(o, c_w_o16, (j,), x, 1.0, 1024)
            w = min(nb_c * CHUNK, seq)
            c_kp.append(k_p[n_prompt - w:].reshape(batch, w, kv_c, hd_c))
            c_vp.append(v_p[n_prompt - w:].reshape(batch, w, kv_c, hd_c))
            c_ks.append(k_s)
            c_vs.append(v_s)
        x = half_ffn(x, norm_g[i, 2], i, 1, split_rows=(i == depth - 1))

    y_prompt = x[0].reshape(batch, seq, d)
    y_sample = x[1].reshape(n_seq, t_dec, d)
    return (y_prompt, y_sample,
            jnp.stack(a_kp), jnp.stack(a_vp),
            _roll_caches(cache_a_k, a_ks, t_dec), _roll_caches(cache_a_v, a_vs, t_dec),
            jnp.stack(b_kp), jnp.stack(b_vp), jnp.stack(b_lp), jnp.stack(b_ks), jnp.stack(b_vs), jnp.stack(b_ls),
            jnp.stack(c_kp), jnp.stack(c_vp),
            _roll_caches(cache_c_k, c_ks, t_dec), _roll_caches(cache_c_v, c_vs, t_dec))
```

```python
import functools
import math

import jax
import jax.numpy as jnp
from jax import lax
from jax.experimental import pallas as pl
from jax.experimental.pallas import tpu as pltpu

CHUNK = 64
NB_A = 8
EPS = 1e-6
NEG_INF = -1e30
FFN_RES = 0.5
T5_MAX_DIST = 128
LOG2E = math.log2(math.e)

V7X_LANES = 128
V7X_SUBLANES = 8
V7X_VMEM_BYTES = 64 * 1024 * 1024
VMEM_HEADROOM_BYTES = 14 * 1024 * 1024

F32 = jnp.float32
BF16 = jnp.bfloat16


def _params(semantics, block_bytes):
    limit = min(2 * block_bytes + VMEM_HEADROOM_BYTES, V7X_VMEM_BYTES - 2 * 1024 * 1024)
    return pltpu.CompilerParams(dimension_semantics=semantics, vmem_limit_bytes=int(limit))


def _row_tile(m, want):
    t = want
    while m % t:
        t //= 2
    return t


def _dot(a, b):
    return jnp.dot(a, b, preferred_element_type=F32)


def _dot_nt(a, b):
    return lax.dot_general(a, b, (((1,), (1,)), ((), ())), preferred_element_type=F32)


def _rms_kernel(x_ref, g_ref, o_ref):
    x = x_ref[...]
    ms = jnp.mean(x * x, axis=-1, keepdims=True)
    o_ref[...] = ((x * lax.rsqrt(ms + EPS)) * g_ref[...]).astype(o_ref.dtype)


def rms_norm_rows(x, g):
    m, d = x.shape
    tm = _row_tile(m, 256)
    return pl.pallas_call(
        _rms_kernel,
        grid=(m // tm,),
        in_specs=[pl.BlockSpec((tm, d), lambda i: (i, 0)),
                  pl.BlockSpec((1, d), lambda i: (0, 0))],
        out_specs=pl.BlockSpec((tm, d), lambda i: (i, 0)),
        out_shape=jax.ShapeDtypeStruct((m, d), BF16),
        compiler_params=_params(("arbitrary",), tm * d * 6),
        name="rms_norm",
    )(x, g.reshape(1, d).astype(F32))


def _ffn_in_kernel(x_ref, wg_ref, wu_ref, o_ref):
    x = x_ref[...]
    gate = _dot(x, wg_ref[...].astype(BF16))
    up = _dot(x, wu_ref[...].astype(BF16))
    o_ref[...] = (gate * jax.nn.sigmoid(gate) * up).astype(o_ref.dtype)


def _ffn_tile(f):
    for tf in (256, 128):
        if f % tf == 0:
            return tf
    raise ValueError(f"D_FF={f} must be a multiple of {V7X_LANES}")


def _weight_spec(lead, k, tn, col):
    return pl.BlockSpec((None,) * len(lead) + (k, tn), lambda i, j: (*lead, 0, col(j)))


def ffn_in(xn, w_in, lead):
    m, k = xn.shape
    f = w_in.shape[-1] // 2
    tf = _ffn_tile(f)
    n_f = f // tf
    tm = _row_tile(m, 1024)
    blk = tm * k * 2 + 2 * k * tf * (4 + 1) + tm * tf * 2 + tm * 2 * tf * 4
    return pl.pallas_call(
        _ffn_in_kernel,
        grid=(m // tm, n_f),
        in_specs=[pl.BlockSpec((tm, k), lambda i, j: (i, 0)),
                  _weight_spec(lead, k, tf, lambda j: j),
                  _weight_spec(lead, k, tf, lambda j: n_f + j)],
        out_specs=pl.BlockSpec((tm, tf), lambda i, j: (i, j)),
        out_shape=jax.ShapeDtypeStruct((m, f), BF16),
        compiler_params=_params(("arbitrary", "arbitrary"), blk),
        name="ffn_in",
    )(xn, w_in, w_in)


def _mm_res_kernel(a_ref, w_ref, r_ref, o_ref, *, scale):
    acc = _dot(a_ref[...], w_ref[...].astype(BF16))
    o_ref[...] = r_ref[...] + scale * acc


def matmul_residual(a, w, lead, res, scale, tm_want, row0=0, rows=None):
    m, k = a.shape
    n = w.shape[-1]
    rows = m if rows is None else rows
    tm = _row_tile(math.gcd(row0, rows) if row0 else rows, tm_want)
    tn = _row_tile(n, 512)
    i0 = row0 // tm
    blk = tm * k * 2 + k * tn * (w.dtype.itemsize + 1) + 3 * tm * tn * 4
    return pl.pallas_call(
        functools.partial(_mm_res_kernel, scale=scale),
        grid=(rows // tm, n // tn),
        in_specs=[pl.BlockSpec((tm, k), lambda i, j: (i0 + i, 0)),
                  _weight_spec(lead, k, tn, lambda j: j),
                  pl.BlockSpec((tm, tn), lambda i, j: (i0 + i, j))],
        out_specs=pl.BlockSpec((tm, tn), lambda i, j: (i, j)),
        out_shape=jax.ShapeDtypeStruct((rows, n), F32),
        compiler_params=_params(("arbitrary", "arbitrary"), blk),
        name="matmul_residual",
    )(a, w, res)


def _head_rms(acc, hd, normalise):
    tn = acc.shape[1]
    parts = []
    if hd == V7X_LANES:
        for h in range(tn // hd):
            xh = acc[:, h * hd:(h + 1) * hd]
            ms = jnp.mean(xh * xh, axis=-1, keepdims=True)
            parts.append(xh * jnp.where(normalise, lax.rsqrt(ms + EPS), 1.0))
    else:
        assert hd * 2 == V7X_LANES
        lo = lax.broadcasted_iota(jnp.int32, (1, V7X_LANES), 1) < hd
        for h in range(tn // V7X_LANES):
            xh = acc[:, h * V7X_LANES:(h + 1) * V7X_LANES]
            x2 = xh * xh
            s_lo = jnp.sum(jnp.where(lo, x2, 0.0), axis=-1, keepdims=True)
            s_hi = jnp.sum(jnp.where(lo, 0.0, x2), axis=-1, keepdims=True)
            ms = jnp.where(lo, s_lo, s_hi) * (1.0 / hd)
            parts.append(xh * jnp.where(normalise, lax.rsqrt(ms + EPS), 1.0))
    return jnp.concatenate(parts, axis=-1)


def _qkv_kernel(x_ref, w_ref, g_ref, o16_ref, o32_ref, *, hd, n_norm_tiles, row_chunks):
    normalise = pl.program_id(1) < n_norm_tiles
    g = g_ref[...]
    w = w_ref[...].astype(BF16)
    rc = x_ref.shape[0] // row_chunks
    for c in range(row_chunks):
        rows = slice(c * rc, (c + 1) * rc)
        y = _head_rms(_dot(x_ref[rows, :], w), hd, normalise) * g
        o16_ref[rows, :] = y.astype(o16_ref.dtype)
        o32_ref[rows, :] = y


def qkv_project(xn, row0, rows, w, lead, gains, hd, q_width, kv_width):
    k = xn.shape[1]
    n = q_width + 2 * kv_width
    tm = _row_tile(math.gcd(row0, rows) if row0 else rows, 1024)
    tn = _row_tile(math.gcd(q_width, kv_width), 512)
    n_q, n_k = q_width // tn, kv_width // tn
    i0 = row0 // tm
    row_chunks = 4 if tm % 64 == 0 else 1
    blk = tm * k * 2 + k * tn * (w.dtype.itemsize + 1) + tm * tn * (2 + 4 + 4)
    return pl.pallas_call(
        functools.partial(_qkv_kernel, hd=hd, n_norm_tiles=n_q + n_k, row_chunks=row_chunks),
        grid=(rows // tm, n // tn),
        in_specs=[pl.BlockSpec((tm, k), lambda i, j: (i0 + i, 0)),
                  _weight_spec(lead, k, tn, lambda j: j),
                  pl.BlockSpec((1, tn), lambda i, j: (0, j))],
        out_specs=[pl.BlockSpec((tm, tn), lambda i, j: (i, j)),
                   pl.BlockSpec((tm, tn), lambda i, j: (i, jnp.maximum(j - n_q, 0)))],
        out_shape=[jax.ShapeDtypeStruct((rows, n), BF16),
                   jax.ShapeDtypeStruct((rows, 2 * kv_width), F32)],
        compiler_params=_params(("arbitrary", "arbitrary"), blk),
        name="qkv_project",
    )(xn, w, gains)


def _fgate_kernel(x_ref, w_ref, b_ref, o_ref):
    z = _dot(x_ref[...], w_ref[...]) + b_ref[...]
    o_ref[...] = jnp.minimum(z, 0.0) - jnp.log1p(jnp.exp(-jnp.abs(z)))


def forget_gate(xn, w_pad, b_pad):
    m, k = xn.shape
    n = w_pad.shape[1]
    tm = _row_tile(m, 1024)
    blk = tm * k * 2 + k * n * 2 + tm * n * 8
    return pl.pallas_call(
        _fgate_kernel,
        grid=(m // tm,),
        in_specs=[pl.BlockSpec((tm, k), lambda i: (i, 0)),
                  pl.BlockSpec((k, n), lambda i: (0, 0)),
                  pl.BlockSpec((1, n), lambda i: (0, 0))],
        out_specs=pl.BlockSpec((tm, n), lambda i: (i, 0)),
        out_shape=jax.ShapeDtypeStruct((m, n), F32),
        compiler_params=_params(("arbitrary",), blk),
        name="forget_gate",
    )(xn, w_pad, b_pad)


def _cumsum_kernel(x_ref, o_ref, *, blk):
    length, width = x_ref.shape
    row = lax.broadcasted_iota(jnp.int32, (blk, blk), 0)
    col = lax.broadcasted_iota(jnp.int32, (blk, blk), 1)
    tri = (row >= col).astype(F32)

    def body(i, carry):
        start = pl.multiple_of(i * blk, blk)
        xs = x_ref[pl.ds(start, blk), :]
        c = jnp.dot(tri, xs, precision=lax.Precision.HIGHEST, preferred_element_type=F32) + carry
        o_ref[pl.ds(start, blk), :] = c
        return c[blk - 1:blk, :]

    lax.fori_loop(0, length // blk, body, jnp.zeros((1, width), F32))


def cumsum_time(x, n_seq, length, blk):
    width = x.shape[1]
    return pl.pallas_call(
        functools.partial(_cumsum_kernel, blk=blk),
        grid=(n_seq,),
        in_specs=[pl.BlockSpec((length, width), lambda i: (i, 0))],
        out_specs=pl.BlockSpec((None, length, width), lambda i: (i, 0, 0)),
        out_shape=jax.ShapeDtypeStruct((n_seq, length, width), F32),
        compiler_params=_params(("arbitrary",), 2 * length * width * 4),
        name="cumsum_time",
    )(x)


def _head_rows(ref, head, hd):
    if len(ref.shape) == 3:
        return ref[:, head, :]
    return ref[:, head * hd:(head + 1) * hd]


def _band_kernel(*refs, n_kblk, kv_heads, group, hd, scale2, has_sink, first_blk, heads_per_step):
    q_ref = refs[0]
    k_refs = refs[1:1 + n_kblk]
    v_refs = refs[1 + n_kblk:1 + 2 * n_kblk]
    bias_ref = refs[1 + 2 * n_kblk]
    sink_ref = refs[2 + 2 * n_kblk] if has_sink else None
    o_ref = refs[-1]

    hg = pl.program_id(0)
    qi = pl.program_id(1)
    widths = [r.shape[0] for r in k_refs]
    logits = []
    for kv in range(kv_heads):
        kparts = [_head_rows(r, kv, hd).astype(BF16) for r in k_refs]
        for g in range(group):
            h = kv * group + g
            qh = q_ref[:, h * hd:(h + 1) * hd]
            sparts = []
            for t, kp in enumerate(kparts):
                st = _dot_nt(qh, kp)
                if first_blk is not None:
                    st = jnp.where(first_blk(qi, t) >= 0, st, NEG_INF)
                sparts.append(st)
            s = jnp.concatenate(sparts, axis=1) if len(sparts) > 1 else sparts[0]
            logits.append(s * scale2 + bias_ref[h])
    outs = []
    for kv in range(kv_heads):
        vparts = [_head_rows(r, kv, hd).astype(BF16) for r in v_refs]
        for g in range(group):
            h = kv * group + g
            z = logits[h]
            m = jnp.max(z, axis=-1, keepdims=True)
            if has_sink:
                sk = sink_ref[hg * heads_per_step + h] * LOG2E
                m = jnp.maximum(m, sk)
            e = jnp.exp2(z - m)
            denom = jnp.sum(e, axis=-1, keepdims=True)
            if has_sink:
                denom = denom + jnp.exp2(sk - m)
            p = (e / denom).astype(BF16)
            off = 0
            acc = None
            for w_t, vp in zip(widths, vparts):
                part = _dot(p[:, off:off + w_t], vp)
                acc = part if acc is None else acc + part
                off += w_t
            outs.append(acc)
    per_store = V7X_LANES // hd
    for c in range(len(outs) // per_store):
        piece = outs[c * per_store:(c + 1) * per_store]
        val = jnp.concatenate(piece, axis=1) if per_store > 1 else piece[0]
        o_ref[:, c * V7X_LANES:(c + 1) * V7X_LANES] = val.astype(o_ref.dtype)


def _drop_alias_ref(kern):
    def body(*refs):
        return kern(*refs[:-2], refs[-1])
    return body


def band_attention(q_arr, q_spec, k_arrs, k_specs, v_arrs, v_specs, bias, sinks, o_prev, o_shape, o_spec,
                   grid, *, kv_heads, group, hd, first_blk, block_bytes):
    heads_per_step = kv_heads * group
    n_kblk = len(k_arrs)
    has_sink = sinks is not None
    tq, nk = bias.shape[1], bias.shape[2]
    in_specs = [q_spec, *k_specs, *v_specs, pl.BlockSpec((heads_per_step, tq, nk), lambda a, b: (a, 0, 0))]
    args = [q_arr, *k_arrs, *v_arrs, bias]
    if has_sink:
        in_specs.append(pl.BlockSpec(memory_space=pltpu.SMEM))
        args.append(sinks)
    body = functools.partial(_band_kernel, n_kblk=n_kblk, kv_heads=kv_heads, group=group, hd=hd,
                             scale2=hd ** -0.5 * LOG2E, has_sink=has_sink, first_blk=first_blk,
                             heads_per_step=heads_per_step)
    aliases = {}
    if o_prev is not None:
        in_specs.append(pl.BlockSpec(memory_space=pl.ANY))
        args.append(o_prev)
        aliases = {len(args) - 1: 0}
        body = _drop_alias_ref(body)
    return pl.pallas_call(
        body,
        grid=grid,
        in_specs=in_specs,
        out_specs=o_spec,
        out_shape=o_shape,
        input_output_aliases=aliases,
        compiler_params=_params(("arbitrary", "arbitrary"), block_bytes),
        name="band_attention",
    )(*args)


def _head_column(cum_blk, h):
    lane = lax.broadcasted_iota(jnp.int32, (1, V7X_LANES), 1)
    return jnp.sum(jnp.where(lane == h, cum_blk, 0.0), axis=-1, keepdims=True)


def _fox_prompt_kernel(q_ref, k_ref, v_ref, cum_ref, ck_ref, o_ref, *, tq, scale2):
    h = pl.program_id(0)
    qb = pl.program_id(1)
    q = q_ref[...]
    hd = q.shape[1]
    cq = _head_column(cum_ref[...], h) * LOG2E

    def step(kb, carry, diagonal):
        m, l, acc = carry
        start = pl.multiple_of(kb * tq, tq)
        kblk = k_ref[pl.ds(start, tq), :]
        vblk = v_ref[pl.ds(start, tq), :]
        ck = ck_ref[kb] * LOG2E
        z = _dot_nt(q, kblk) * scale2 + (cq - ck)
        if diagonal:
            row = lax.broadcasted_iota(jnp.int32, (tq, tq), 0)
            col = lax.broadcasted_iota(jnp.int32, (tq, tq), 1)
            z = jnp.where(col <= row, z, NEG_INF)
        m_new = jnp.maximum(m, jnp.max(z, axis=-1, keepdims=True))
        alpha = jnp.exp2(m - m_new)
        p = jnp.exp2(z - m_new)
        l_new = alpha * l + jnp.sum(p, axis=-1, keepdims=True)
        acc_new = alpha * acc + _dot(p.astype(BF16), vblk)
        return m_new, l_new, acc_new

    carry = (jnp.full((tq, 1), NEG_INF, F32), jnp.zeros((tq, 1), F32), jnp.zeros((tq, hd), F32))
    carry = lax.fori_loop(0, qb // 2, lambda i, c: step(2 * i + 1, step(2 * i, c, False), False), carry)
    carry = lax.fori_loop((qb // 2) * 2, qb, lambda kb, c: step(kb, c, False), carry)
    _, l, acc = step(qb, carry, True)
    o_ref[...] = (acc / l).astype(o_ref.dtype)


def fox_prompt(qkv16, cum, ck, o_shape, *, heads, hd, seq, tq):
    blk = 2 * seq * hd * 2 + 2 * tq * hd * 2 + tq * V7X_LANES * 4 + 8 * seq * 4
    return pl.pallas_call(
        functools.partial(_fox_prompt_kernel, tq=tq, scale2=hd ** -0.5 * LOG2E),
        grid=(heads, seq // tq),
        in_specs=[pl.BlockSpec((tq, hd), lambda h, qb: (qb, h)),
                  pl.BlockSpec((seq, hd), lambda h, qb: (0, heads + h)),
                  pl.BlockSpec((seq, hd), lambda h, qb: (0, 2 * heads + h)),
                  pl.BlockSpec((None, tq, V7X_LANES), lambda h, qb: (0, qb, 0)),
                  pl.BlockSpec((None, seq // tq, 1, tq), lambda h, qb: (h, 0, 0, 0))],
        out_specs=pl.BlockSpec((tq, hd), lambda h, qb: (qb, h)),
        out_shape=o_shape,
        compiler_params=pltpu.CompilerParams(dimension_semantics=("arbitrary", "arbitrary"),
                                             vmem_limit_bytes=int(min(2 * blk + 32 * 1024 * 1024,
                                                                      V7X_VMEM_BYTES - 2 * 1024 * 1024))),
        name="fox_prompt",
    )(qkv16, qkv16, qkv16, cum, ck)


def _fox_sample_kernel(q_ref, kc_ref, vc_ref, kn_ref, vn_ref, cum_ref, ck_ref, prev_ref, o_ref, *,
                       heads_per_step, hd, past, scale2):
    del prev_ref
    hg = pl.program_id(0)
    t = q_ref.shape[0]
    row = lax.broadcasted_iota(jnp.int32, (t, past + t), 0)
    col = lax.broadcasted_iota(jnp.int32, (t, past + t), 1)
    causal = col <= row + past
    logits = []
    for hh in range(heads_per_step):
        sl = slice(hh * hd, (hh + 1) * hd)
        qh = q_ref[:, sl]
        cq = _head_column(cum_ref[...], hg * heads_per_step + hh) * LOG2E
        ck = ck_ref[hh] * LOG2E
        s = jnp.concatenate([_dot_nt(qh, kc_ref[:, hh, :].astype(BF16)), _dot_nt(qh, kn_ref[:, sl])], axis=1)
        logits.append(jnp.where(causal, s * scale2 + (cq - ck), NEG_INF))
    for hh in range(heads_per_step):
        sl = slice(hh * hd, (hh + 1) * hd)
        z = logits[hh]
        m = jnp.max(z, axis=-1, keepdims=True)
        e = jnp.exp2(z - m)
        p = (e / jnp.sum(e, axis=-1, keepdims=True)).astype(BF16)
        o = _dot(p[:, :past], vc_ref[:, hh, :].astype(BF16)) + _dot(p[:, past:], vn_ref[:, sl])
        o_ref[:, sl] = o.astype(o_ref.dtype)


def fox_sample(qkv16_s, cache_k, cache_v, cum, ck, o_prev, row0_blocks, *, n_seq, heads, hd, t_dec, past):
    hps = V7X_SUBLANES
    assert heads % hps == 0
    width = hps * hd
    n_hg = heads // hps
    ls = past + t_dec
    blk = (2 * past * width * 4 + 3 * t_dec * width * 2 + t_dec * V7X_LANES * 4 + hps * 8 * ls * 4
           + 2 * past * width * 2 + 6 * t_dec * ls * 4)
    return pl.pallas_call(
        functools.partial(_fox_sample_kernel, heads_per_step=hps, hd=hd, past=past, scale2=hd ** -0.5 * LOG2E),
        grid=(n_hg, n_seq),
        in_specs=[pl.BlockSpec((t_dec, width), lambda hg, b: (b, hg)),
                  pl.BlockSpec((past, hps, hd), lambda hg, b: (b, hg, 0)),
                  pl.BlockSpec((past, hps, hd), lambda hg, b: (b, hg, 0)),
                  pl.BlockSpec((t_dec, width), lambda hg, b: (b, n_hg + hg)),
                  pl.BlockSpec((t_dec, width), lambda hg, b: (b, 2 * n_hg + hg)),
                  pl.BlockSpec((None, t_dec, V7X_LANES), lambda hg, b: (b, past // t_dec, 0)),
                  pl.BlockSpec((None, hps, 1, ls), lambda hg, b: (b, hg, 0, 0)),
                  pl.BlockSpec(memory_space=pl.ANY)],
        out_specs=pl.BlockSpec((t_dec, width), lambda hg, b: (row0_blocks + b, hg)),
        out_shape=jax.ShapeDtypeStruct(o_prev.shape, o_prev.dtype),
        input_output_aliases={7: 0},
        compiler_params=_params(("arbitrary", "arbitrary"), blk),
        name="fox_sample",
    )(qkv16_s, cache_k, cache_v, qkv16_s, qkv16_s, cum, ck, o_prev)


def _toeplitz(vals, tq, nk):
    w = tq + nk
    flat = jnp.tile(vals, (1, tq))[:, :tq * w]
    return flat.reshape(vals.shape[0], tq, w)[:, :, :nk]


def _band_bias(rel_to_index, table, q0, k0, tq, nk, n_prev):
    d = jnp.arange(tq + nk + 1)
    j_minus_i = jnp.where(d < nk, d, d - (tq + nk + 1))
    vals = table.astype(F32)[rel_to_index((q0 - k0) - j_minus_i)].T * LOG2E
    qc = (q0 + jnp.arange(tq))[:, None] // CHUNK
    kc = (k0 + jnp.arange(nk))[None, :] // CHUNK
    valid = (kc <= qc) & (kc >= qc - n_prev)
    return jnp.where(valid[None], _toeplitz(vals, tq, nk), NEG_INF)


def _clip_index(clip, rel):
    return jnp.clip(rel, -clip, clip) + clip


def _t5_bucket(n_buckets, rel):
    half = n_buckets // 2
    max_exact = half // 2
    n = jnp.abs(rel)
    ratio = jnp.maximum(n, max_exact).astype(F32) / max_exact
    large = max_exact + (jnp.log(ratio) / math.log(T5_MAX_DIST / max_exact) * (half - max_exact)).astype(jnp.int32)
    large = jnp.minimum(large, half - 1)
    return jnp.where(rel < 0, half, 0) + jnp.where(n < max_exact, n, large)


def _band_mixer(qkv16_p, qkv16_s, m_total, n_seq, t_dec, cache_k, cache_v, rel_to_index, table, sinks, *,
                d_q, kv_heads_total, group, hd, n_prev, past_len, kv_per_step, tq):
    n_prompt = qkv16_p.shape[0]
    kvw = kv_heads_total * hd
    qw_step = kv_per_step * group * hd
    kw_step = kv_per_step * hd
    n_hg = kv_heads_total // kv_per_step
    heads_step = kv_per_step * group
    win = n_prev * CHUNK
    tkb = math.gcd(tq, win)
    n_kblk = (tq + win) // tkb
    n_back = win // tkb
    ratio = tq // tkb
    k_col0 = d_q // kw_step
    v_col0 = (d_q + kvw) // kw_step
    o_shape = jax.ShapeDtypeStruct((m_total, d_q), BF16)

    bias_p = _band_bias(rel_to_index, table, win, 0, tq, tq + win, n_prev)

    def first_blk(qi, t):
        return qi * ratio - n_back + t

    def kmap(t, col0):
        return lambda hg, qi: (jnp.maximum(qi * ratio - n_back + t, 0), col0 + hg)

    k_specs = [pl.BlockSpec((tkb, kw_step), kmap(t, k_col0)) for t in range(n_kblk)]
    v_specs = [pl.BlockSpec((tkb, kw_step), kmap(t, v_col0)) for t in range(n_kblk)]
    blk = (tq * qw_step * 2 * 2 + 2 * n_kblk * tkb * kw_step * 2 + heads_step * tq * (tq + win) * 4
           + 8 * tq * (tq + win) * 4)
    o = band_attention(
        qkv16_p, pl.BlockSpec((tq, qw_step), lambda hg, qi: (qi, hg)),
        [qkv16_p] * n_kblk, k_specs, [qkv16_p] * n_kblk, v_specs, bias_p, sinks, None, o_shape,
        pl.BlockSpec((tq, qw_step), lambda hg, qi: (qi, hg)),
        (n_hg, n_prompt // tq), kv_heads=kv_per_step, group=group, hd=hd, first_blk=first_blk, block_bytes=blk)

    w = cache_k.shape[1]
    bias_s = _band_bias(rel_to_index, table, past_len, past_len - w, t_dec, w + t_dec, n_prev)
    row0 = n_prompt // t_dec
    if hd == V7X_LANES and kv_heads_total % V7X_SUBLANES == 0:
        kv_per_step = V7X_SUBLANES
        qw_step, kw_step, heads_step = kv_per_step * group * hd, kv_per_step * hd, kv_per_step * group
        n_hg, k_col0, v_col0 = kv_heads_total // kv_per_step, d_q // kw_step, (d_q + kvw) // kw_step
        ck2 = cache_k.reshape(n_seq * w, kv_heads_total, hd)
        cv2 = cache_v.reshape(n_seq * w, kv_heads_total, hd)
        cache_spec = pl.BlockSpec((w, kv_per_step, hd), lambda hg, b: (b, hg, 0))
    else:
        ck2 = cache_k.reshape(n_seq * w, kvw)
        cv2 = cache_v.reshape(n_seq * w, kvw)
        cache_spec = pl.BlockSpec((w, kw_step), lambda hg, b: (b, hg))
    k_specs = [cache_spec, pl.BlockSpec((t_dec, kw_step), lambda hg, b: (b, k_col0 + hg))]
    v_specs = [cache_spec, pl.BlockSpec((t_dec, kw_step), lambda hg, b: (b, v_col0 + hg))]
    blk = (t_dec * qw_step * 4 + 2 * w * kw_step * 4 + 2 * t_dec * kw_step * 2
           + heads_step * t_dec * (w + t_dec) * 4 + 8 * t_dec * (w + t_dec) * 4 + 2 * w * kw_step * 2)
    o = band_attention(
        qkv16_s, pl.BlockSpec((t_dec, qw_step), lambda hg, b: (b, hg)),
        [ck2, qkv16_s], k_specs, [cv2, qkv16_s], v_specs, bias_s, sinks, o, o_shape,
        pl.BlockSpec((t_dec, qw_step), lambda hg, b: (row0 + b, hg)),
        (n_hg, n_seq), kv_heads=kv_per_step, group=group, hd=hd, first_blk=None, block_bytes=blk)
    return o


def _roll_caches(caches, new_rows, t_dec):
    new = jnp.stack(new_rows).reshape(caches.shape[0], caches.shape[1], t_dec, *caches.shape[3:])
    return jnp.concatenate([caches[:, :, t_dec:], new], axis=2)


def _qk_gains(qk_g, n_q_heads, n_k_heads, v_width):
    g = jnp.concatenate([jnp.tile(qk_g[0], n_q_heads), jnp.tile(qk_g[1], n_k_heads), jnp.ones((v_width,), F32)])
    return g.reshape(1, -1).astype(F32)


def kernel(x_prompt, x_sample, cache_a_k, cache_a_v, cache_b_k, cache_b_v, cache_b_logf, cache_c_k, cache_c_v,
           norm_g, ffn_w_in, ffn_w_out, a_w_qkv, a_w_o, a_qk_g, a_rel_bias, b_w_qkvf, b_f_bias, b_w_o, b_qk_g,
           c_w_qkv, c_w_o, c_qk_g, c_sinks, t5_bias):
    batch, seq, d = x_prompt.shape
    n_seq, t_dec, _ = x_sample.shape
    assert batch == 1 and t_dec == CHUNK and seq % 256 == 0
    depth = norm_g.shape[0]
    past_len = cache_b_k.shape[2]
    assert past_len % CHUNK == 0
    n_prompt = batch * seq
    n_samp = n_seq * t_dec
    m = n_prompt + n_samp

    h_a, hd_a = a_rel_bias.shape[2], a_qk_g.shape[2]
    h_b, hd_b = b_f_bias.shape[1], b_qk_g.shape[2]
    h_c, hd_c = c_sinks.shape[1], c_qk_g.shape[2]
    kv_c = cache_c_k.shape[3]
    g_c = h_c // kv_c
    nb_c = cache_c_k.shape[2] // CHUNK
    assert hd_a == V7X_LANES and hd_b == V7X_LANES

    x = jnp.concatenate([x_prompt.reshape(n_prompt, d), x_sample.reshape(n_samp, d)], axis=0)

    ffn_w_out16 = ffn_w_out.astype(BF16)

    def half_ffn(x, g, layer, half, split_rows=False):
        xn = rms_norm_rows(x, g)
        h = ffn_in(xn, ffn_w_in, (layer, half))
        if not split_rows:
            return matmul_residual(h, ffn_w_out16, (layer, half), x, FFN_RES, 512)
        return (matmul_residual(h, ffn_w_out16, (layer, half), x, FFN_RES, 512, 0, n_prompt),
                matmul_residual(h, ffn_w_out16, (layer, half), x, FFN_RES, 512, n_prompt, n_samp))

    a_kp, a_vp, a_ks, a_vs = [], [], [], []
    b_kp, b_vp, b_lp, b_ks, b_vs, b_ls = [], [], [], [], [], []
    c_kp, c_vp, c_ks, c_vs = [], [], [], []

    for i in range(depth):
        kind, j = i % 3, i // 3
        x = half_ffn(x, norm_g[i, 0], i, 0)
        u = rms_norm_rows(x, norm_g[i, 1])
        if kind == 0:
            dq = h_a * hd_a
            gains = _qk_gains(a_qk_g[j], h_a, h_a, dq)
            qkv_p, kv_p = qkv_project(u, 0, n_prompt, a_w_qkv, (j,), gains, hd_a, dq, dq)
            qkv_s, kv_s = qkv_project(u, n_prompt, n_samp, a_w_qkv, (j,), gains, hd_a, dq, dq)
            k_p, v_p, k_s, v_s = kv_p[:, :dq], kv_p[:, dq:], kv_s[:, :dq], kv_s[:, dq:]
            clip = (a_rel_bias.shape[1] - 1) // 2
            o = _band_mixer(qkv_p, qkv_s, m, n_seq, t_dec, cache_a_k[j], cache_a_v[j],
                            functools.partial(_clip_index, clip), a_rel_bias[j], None,
                            d_q=dq, kv_heads_total=h_a, group=1, hd=hd_a, n_prev=NB_A, past_len=past_len,
                            kv_per_step=4, tq=256)
            x = matmul_residual(o, a_w_o, (j,), x, 1.0, 1024)
            w = min(NB_A * CHUNK, seq)
            a_kp.append(k_p[n_prompt - w:].reshape(batch, w, h_a, hd_a))
            a_vp.append(v_p[n_prompt - w:].reshape(batch, w, h_a, hd_a))
            a_ks.append(k_s)
            a_vs.append(v_s)
        elif kind == 1:
            dq = h_b * hd_b
            gains = _qk_gains(b_qk_g[j], h_b, h_b, dq)
            qkv_p, kv_p = qkv_project(u, 0, n_prompt, b_w_qkvf, (j,), gains, hd_b, dq, dq)
            qkv_s, kv_s = qkv_project(u, n_prompt, n_samp, b_w_qkvf, (j,), gains, hd_b, dq, dq)
            k_p, v_p, k_s, v_s = kv_p[:, :dq], kv_p[:, dq:], kv_s[:, :dq], kv_s[:, dq:]
            w_f = jnp.pad(b_w_qkvf[j, :, 3 * dq:].astype(BF16), ((0, 0), (0, V7X_LANES - h_b)))
            b_f = jnp.pad(b_f_bias[j].astype(F32), (0, V7X_LANES - h_b)).reshape(1, V7X_LANES)
            logf = forget_gate(u, w_f, b_f)
            tq = 1024 if seq % 1024 == 0 else 256
            cum_p = cumsum_time(logf, batch, seq, 256)
            ck = cum_p[0, :, :h_b].T.reshape(h_b, seq // tq, 1, tq)
            o = fox_prompt(qkv_p, cum_p, ck, jax.ShapeDtypeStruct((m, dq), BF16),
                           heads=h_b, hd=hd_b, seq=seq, tq=tq)
            ls = past_len + t_dec
            logf_all = jnp.concatenate(
                [jnp.pad(cache_b_logf[j].astype(F32), ((0, 0), (0, 0), (0, V7X_LANES - h_b))),
                 logf[n_prompt:].reshape(n_seq, t_dec, V7X_LANES)], axis=1)
            cum_s = cumsum_time(logf_all.reshape(n_seq * ls, V7X_LANES), n_seq, ls, CHUNK)
            ck = cum_s[:, :, :h_b].transpose(0, 2, 1)[:, :, None, :]
            o = fox_sample(qkv_s, cache_b_k[j].reshape(n_seq * past_len, h_b, hd_b),
                           cache_b_v[j].reshape(n_seq * past_len, h_b, hd_b), cum_s, ck, o, n_prompt // t_dec,
                           n_seq=n_seq, heads=h_b, hd=hd_b, t_dec=t_dec, past=past_len)
            x = matmul_residual(o, b_w_o, (j,), x, 1.0, 1024)
            b_kp.append(k_p.reshape(batch, seq, h_b, hd_b))
            b_vp.append(v_p.reshape(batch, seq, h_b, hd_b))
            b_lp.append(logf[:n_prompt, :h_b].reshape(batch, seq, h_b))
            b_ks.append(k_s.reshape(n_seq, t_dec, h_b, hd_b))
            b_vs.append(v_s.reshape(n_seq, t_dec, h_b, hd_b))
            b_ls.append(logf[n_prompt:, :h_b].reshape(n_seq, t_dec, h_b))
        else:
            dq, kvw = h_c * hd_c, kv_c * hd_c
            gains = _qk_gains(c_qk_g[j], h_c, kv_c, kvw)
            qkv_p, kv_p = qkv_project(u, 0, n_prompt, c_w_qkv, (j,), gains, hd_c, dq, kvw)
            qkv_s, kv_s = qkv_project(u, n_prompt, n_samp, c_w_qkv, (j,), gains, hd_c, dq, kvw)
            k_p, v_p, k_s, v_s = kv_p[:, :kvw], kv_p[:, kvw:], kv_s[:, :kvw], kv_s[:, kvw:]
            o = _band_mixer(qkv_p, qkv_s, m, n_seq, t_dec, cache_c_k[j], cache_c_v[j],
                            functools.partial(_t5_bucket, t5_bias.shape[0]), t5_bias, c_sinks[j].astype(F32),
                            d_q=dq, kv_heads_total=kv_c, group=g_c, hd=hd_c, n_prev=nb_c, past_len=past_len,
                            kv_per_step=2, tq=256)
            x = matmul_residual(o, c_w_o, (j,), x, 1.0, 1024)
            w = min(nb_c * CHUNK, seq)
            c_kp.append(k_p[n_prompt - w:].reshape(batch, w, kv_c, hd_c))
            c_vp.append(v_p[n_prompt - w:].reshape(batch, w, kv_c, hd_c))
            c_ks.append(k_s)
            c_vs.append(v_s)
        x = half_ffn(x, norm_g[i, 2], i, 1, split_rows=(i == depth - 1))

    y_prompt = x[0].reshape(batch, seq, d)
    y_sample = x[1].reshape(n_seq, t_dec, d)
    return (y_prompt, y_sample,
            jnp.stack(a_kp), jnp.stack(a_vp),
            _roll_caches(cache_a_k, a_ks, t_dec), _roll_caches(cache_a_v, a_vs, t_dec),
            jnp.stack(b_kp), jnp.stack(b_vp), jnp.stack(b_lp), jnp.stack(b_ks), jnp.stack(b_vs), jnp.stack(b_ls),
            jnp.stack(c_kp), jnp.stack(c_vp),
            _roll_caches(cache_c_k, c_ks, t_dec), _roll_caches(cache_c_v, c_vs, t_dec))
```

```python
import functools
import math

import jax
import jax.numpy as jnp
from jax import lax
from jax.experimental import pallas as pl
from jax.experimental.pallas import tpu as pltpu

CHUNK = 64
NB_A = 8
EPS = 1e-6
NEG_INF = -1e30
FFN_RES = 0.5
T5_MAX_DIST = 128
LOG2E = math.log2(math.e)

V7X_LANES = 128
V7X_SUBLANES = 8
V7X_VMEM_BYTES = 64 * 1024 * 1024
VMEM_HEADROOM_BYTES = 14 * 1024 * 1024

F32 = jnp.float32
BF16 = jnp.bfloat16


def _params(semantics, block_bytes):
    limit = min(2 * block_bytes + VMEM_HEADROOM_BYTES, V7X_VMEM_BYTES - 2 * 1024 * 1024)
    return pltpu.CompilerParams(dimension_semantics=semantics, vmem_limit_bytes=int(limit))


def _row_tile(m, want):
    t = want
    while m % t:
        t //= 2
    return t


def _dot(a, b):
    return jnp.dot(a, b, preferred_element_type=F32)


def _dot_nt(a, b):
    return lax.dot_general(a, b, (((1,), (1,)), ((), ())), preferred_element_type=F32)


def _rms_kernel(x_ref, g_ref, o_ref):
    x = x_ref[...]
    ms = jnp.mean(x * x, axis=-1, keepdims=True)
    o_ref[...] = ((x * lax.rsqrt(ms + EPS)) * g_ref[...]).astype(o_ref.dtype)


def rms_norm_rows(x, g):
    m, d = x.shape
    tm = _row_tile(m, 256)
    return pl.pallas_call(
        _rms_kernel,
        grid=(m // tm,),
        in_specs=[pl.BlockSpec((tm, d), lambda i: (i, 0)),
                  pl.BlockSpec((1, d), lambda i: (0, 0))],
        out_specs=pl.BlockSpec((tm, d), lambda i: (i, 0)),
        out_shape=jax.ShapeDtypeStruct((m, d), BF16),
        compiler_params=_params(("arbitrary",), tm * d * 6),
        name="rms_norm",
    )(x, g.reshape(1, d).astype(F32))


def _ffn_in_kernel(x_ref, wg_ref, wu_ref, o_ref):
    x = x_ref[...]
    gate = _dot(x, wg_ref[...].astype(BF16))
    up = _dot(x, wu_ref[...].astype(BF16))
    o_ref[...] = (gate * jax.nn.sigmoid(gate) * up).astype(o_ref.dtype)


def _ffn_tile(f):
    for tf in (256, 128):
        if f % tf == 0:
            return tf
    raise ValueError(f"D_FF={f} must be a multiple of {V7X_LANES}")


def _weight_spec(lead, k, tn, col):
    return pl.BlockSpec((None,) * len(lead) + (k, tn), lambda i, j: (*lead, 0, col(j)))


def ffn_in(xn, w_in, lead):
    m, k = xn.shape
    f = w_in.shape[-1] // 2
    tf = _ffn_tile(f)
    n_f = f // tf
    tm = _row_tile(m, 1024)
    blk = tm * k * 2 + 2 * k * tf * (4 + 1) + tm * tf * 2 + tm * 2 * tf * 4
    return pl.pallas_call(
        _ffn_in_kernel,
        grid=(m // tm, n_f),
        in_specs=[pl.BlockSpec((tm, k), lambda i, j: (i, 0)),
                  _weight_spec(lead, k, tf, lambda j: j),
                  _weight_spec(lead, k, tf, lambda j: n_f + j)],
        out_specs=pl.BlockSpec((tm, tf), lambda i, j: (i, j)),
        out_shape=jax.ShapeDtypeStruct((m, f), BF16),
        compiler_params=_params(("arbitrary", "arbitrary"), blk),
        name="ffn_in",
    )(xn, w_in, w_in)


def _mm_res_kernel(a_ref, w_ref, r_ref, o_ref, *, scale):
    acc = _dot(a_ref[...], w_ref[...].astype(BF16))
    o_ref[...] = r_ref[...] + scale * acc


def matmul_residual(a, w, lead, res, scale, tm_want, row0=0, rows=None):
    m, k = a.shape
    n = w.shape[-1]
    rows = m if rows is None else rows
    tm = _row_tile(math.gcd(row0, rows) if row0 else rows, tm_want)
    tn = _row_tile(n, 512)
    i0 = row0 // tm
    blk = tm * k * 2 + k * tn * (w.dtype.itemsize + 1) + 3 * tm * tn * 4
    return pl.pallas_call(
        functools.partial(_mm_res_kernel, scale=scale),
        grid=(rows // tm, n // tn),
        in_specs=[pl.BlockSpec((tm, k), lambda i, j: (i0 + i, 0)),
                  _weight_spec(lead, k, tn, lambda j: j),
                  pl.BlockSpec((tm, tn), lambda i, j: (i0 + i, j))],
        out_specs=pl.BlockSpec((tm, tn), lambda i, j: (i, j)),
        out_shape=jax.ShapeDtypeStruct((rows, n), F32),
        compiler_params=_params(("arbitrary", "arbitrary"), blk),
        name="matmul_residual",
    )(a, w, res)


def _head_rms(acc, hd, normalise):
    tn = acc.shape[1]
    parts = []
    if hd == V7X_LANES:
        for h in range(tn // hd):
            xh = acc[:, h * hd:(h + 1) * hd]
            ms = jnp.mean(xh * xh, axis=-1, keepdims=True)
            parts.append(xh * jnp.where(normalise, lax.rsqrt(ms + EPS), 1.0))
    else:
        assert hd * 2 == V7X_LANES
        lo = lax.broadcasted_iota(jnp.int32, (1, V7X_LANES), 1) < hd
        for h in range(tn // V7X_LANES):
            xh = acc[:, h * V7X_LANES:(h + 1) * V7X_LANES]
            x2 = xh * xh
            s_lo = jnp.sum(jnp.where(lo, x2, 0.0), axis=-1, keepdims=True)
            s_hi = jnp.sum(jnp.where(lo, 0.0, x2), axis=-1, keepdims=True)
            ms = jnp.where(lo, s_lo, s_hi) * (1.0 / hd)
            parts.append(xh * jnp.where(normalise, lax.rsqrt(ms + EPS), 1.0))
    return jnp.concatenate(parts, axis=-1)


def _qkv_kernel(x_ref, w_ref, g_ref, o16_ref, o32_ref, *, hd, n_norm_tiles, row_chunks):
    normalise = pl.program_id(1) < n_norm_tiles
    g = g_ref[...]
    w = w_ref[...].astype(BF16)
    rc = x_ref.shape[0] // row_chunks
    for c in range(row_chunks):
        rows = slice(c * rc, (c + 1) * rc)
        y = _head_rms(_dot(x_ref[rows, :], w), hd, normalise) * g
        o16_ref[rows, :] = y.astype(o16_ref.dtype)
        o32_ref[rows, :] = y


def qkv_project(xn, row0, rows, w, lead, gains, hd, q_width, kv_width):
    k = xn.shape[1]
    n = q_width + 2 * kv_width
    tm = _row_tile(math.gcd(row0, rows) if row0 else rows, 1024)
    tn = _row_tile(math.gcd(q_width, kv_width), 512)
    n_q, n_k = q_width // tn, kv_width // tn
    i0 = row0 // tm
    row_chunks = 4 if tm % 64 == 0 else 1
    blk = tm * k * 2 + k * tn * (w.dtype.itemsize + 1) + tm * tn * (2 + 4 + 4)
    return pl.pallas_call(
        functools.partial(_qkv_kernel, hd=hd, n_norm_tiles=n_q + n_k, row_chunks=row_chunks),
        grid=(rows // tm, n // tn),
        in_specs=[pl.BlockSpec((tm, k), lambda i, j: (i0 + i, 0)),
                  _weight_spec(lead, k, tn, lambda j: j),
                  pl.BlockSpec((1, tn), lambda i, j: (0, j))],
        out_specs=[pl.BlockSpec((tm, tn), lambda i, j: (i, j)),
                   pl.BlockSpec((tm, tn), lambda i, j: (i, jnp.maximum(j - n_q, 0)))],
        out_shape=[jax.ShapeDtypeStruct((rows, n), BF16),
                   jax.ShapeDtypeStruct((rows, 2 * kv_width), F32)],
        compiler_params=_params(("arbitrary", "arbitrary"), blk),
        name="qkv_project",
    )(xn, w, gains)


def _fgate_kernel(x_ref, w_ref, b_ref, o_ref):
    z = _dot(x_ref[...], w_ref[...]) + b_ref[...]
    o_ref[...] = jnp.minimum(z, 0.0) - jnp.log1p(jnp.exp(-jnp.abs(z)))


def forget_gate(xn, w_pad, b_pad):
    m, k = xn.shape
    n = w_pad.shape[1]
    tm = _row_tile(m, 1024)
    blk = tm * k * 2 + k * n * 2 + tm * n * 8
    return pl.pallas_call(
        _fgate_kernel,
        grid=(m // tm,),
        in_specs=[pl.BlockSpec((tm, k), lambda i: (i, 0)),
                  pl.BlockSpec((k, n), lambda i: (0, 0)),
                  pl.BlockSpec((1, n), lambda i: (0, 0))],
        out_specs=pl.BlockSpec((tm, n), lambda i: (i, 0)),
        out_shape=jax.ShapeDtypeStruct((m, n), F32),
        compiler_params=_params(("arbitrary",), blk),
        name="forget_gate",
    )(xn, w_pad, b_pad)


def _cumsum_kernel(x_ref, o_ref, *, blk):
    length, width = x_ref.shape
    row = lax.broadcasted_iota(jnp.int32, (blk, blk), 0)
    col = lax.broadcasted_iota(jnp.int32, (blk, blk), 1)
    tri = (row >= col).astype(F32)

    def body(i, carry):
        start = pl.multiple_of(i * blk, blk)
        xs = x_ref[pl.ds(start, blk), :]
        c = jnp.dot(tri, xs, precision=lax.Precision.HIGHEST, preferred_element_type=F32) + carry
        o_ref[pl.ds(start, blk), :] = c
        return c[blk - 1:blk, :]

    lax.fori_loop(0, length // blk, body, jnp.zeros((1, width), F32))


def cumsum_time(x, n_seq, length, blk):
    width = x.shape[1]
    return pl.pallas_call(
        functools.partial(_cumsum_kernel, blk=blk),
        grid=(n_seq,),
        in_specs=[pl.BlockSpec((length, width), lambda i: (i, 0))],
        out_specs=pl.BlockSpec((None, length, width), lambda i: (i, 0, 0)),
        out_shape=jax.ShapeDtypeStruct((n_seq, length, width), F32),
        compiler_params=_params(("arbitrary",), 2 * length * width * 4),
        name="cumsum_time",
    )(x)


def _head_rows(ref, head, hd):
    if len(ref.shape) == 3:
        return ref[:, head, :]
    return ref[:, head * hd:(head + 1) * hd]


def _band_kernel(*refs, n_kblk, kv_heads, group, hd, scale2, has_sink, first_blk, heads_per_step):
    q_ref = refs[0]
    k_refs = refs[1:1 + n_kblk]
    v_refs = refs[1 + n_kblk:1 + 2 * n_kblk]
    bias_ref = refs[1 + 2 * n_kblk]
    sink_ref = refs[2 + 2 * n_kblk] if has_sink else None
    o_ref = refs[-1]

    hg = pl.program_id(0)
    qi = pl.program_id(1)
    widths = [r.shape[0] for r in k_refs]
    logits = []
    for kv in range(kv_heads):
        kparts = [_head_rows(r, kv, hd).astype(BF16) for r in k_refs]
        for g in range(group):
            h = kv * group + g
            qh = q_ref[:, h * hd:(h + 1) * hd]
            sparts = []
            for t, kp in enumerate(kparts):
                st = _dot_nt(qh, kp)
                if first_blk is not None:
                    st = jnp.where(first_blk(qi, t) >= 0, st, NEG_INF)
                sparts.append(st)
            s = jnp.concatenate(sparts, axis=1) if len(sparts) > 1 else sparts[0]
            logits.append(s * scale2 + bias_ref[h])
    outs = []
    for kv in range(kv_heads):
        vparts = [_head_rows(r, kv, hd).astype(BF16) for r in v_refs]
        for g in range(group):
            h = kv * group + g
            z = logits[h]
            m = jnp.max(z, axis=-1, keepdims=True)
            if has_sink:
                sk = sink_ref[hg * heads_per_step + h] * LOG2E
                m = jnp.maximum(m, sk)
            e = jnp.exp2(z - m)
            denom = jnp.sum(e, axis=-1, keepdims=True)
            if has_sink:
                denom = denom + jnp.exp2(sk - m)
            p = (e / denom).astype(BF16)
            off = 0
            acc = None
            for w_t, vp in zip(widths, vparts):
                part = _dot(p[:, off:off + w_t], vp)
                acc = part if acc is None else acc + part
                off += w_t
            outs.append(acc)
    per_store = V7X_LANES // hd
    for c in range(len(outs) // per_store):
        piece = outs[c * per_store:(c + 1) * per_store]
        val = jnp.concatenate(piece, axis=1) if per_store > 1 else piece[0]
        o_ref[:, c * V7X_LANES:(c + 1) * V7X_LANES] = val.astype(o_ref.dtype)


def _drop_alias_ref(kern):
    def body(*refs):
        return kern(*refs[:-2], refs[-1])
    return body


def band_attention(q_arr, q_spec, k_arrs, k_specs, v_arrs, v_specs, bias, sinks, o_prev, o_shape, o_spec,
                   grid, *, kv_heads, group, hd, first_blk, block_bytes):
    heads_per_step = kv_heads * group
    n_kblk = len(k_arrs)
    has_sink = sinks is not None
    tq, nk = bias.shape[1], bias.shape[2]
    in_specs = [q_spec, *k_specs, *v_specs, pl.BlockSpec((heads_per_step, tq, nk), lambda a, b: (a, 0, 0))]
    args = [q_arr, *k_arrs, *v_arrs, bias]
    if has_sink:
        in_specs.append(pl.BlockSpec(memory_space=pltpu.SMEM))
        args.append(sinks)
    body = functools.partial(_band_kernel, n_kblk=n_kblk, kv_heads=kv_heads, group=group, hd=hd,
                             scale2=hd ** -0.5 * LOG2E, has_sink=has_sink, first_blk=first_blk,
                             heads_per_step=heads_per_step)
    aliases = {}
    if o_prev is not None:
        in_specs.append(pl.BlockSpec(memory_space=pl.ANY))
        args.append(o_prev)
        aliases = {len(args) - 1: 0}
        body = _drop_alias_ref(body)
    return pl.pallas_call(
        body,
        grid=grid,
        in_specs=in_specs,
        out_specs=o_spec,
        out_shape=o_shape,
        input_output_aliases=aliases,
        compiler_params=_params(("arbitrary", "arbitrary"), block_bytes),
        name="band_attention",
    )(*args)


def _head_column(cum_blk, h):
    lane = lax.broadcasted_iota(jnp.int32, (1, V7X_LANES), 1)
    return jnp.sum(jnp.where(lane == h, cum_blk, 0.0), axis=-1, keepdims=True)


def _fox_prompt_kernel(q_ref, k_ref, v_ref, cum_ref, ck_ref, o_ref, *, tq, scale2):
    h = pl.program_id(0)
    qb = pl.program_id(1)
    q = q_ref[...]
    hd = q.shape[1]
    cq = _head_column(cum_ref[...], h) * LOG2E

    def step(kb, carry, diagonal):
        m, l, acc = carry
        start = pl.multiple_of(kb * tq, tq)
        kblk = k_ref[pl.ds(start, tq), :]
        vblk = v_ref[pl.ds(start, tq), :]
        ck = ck_ref[kb] * LOG2E
        z = _dot_nt(q, kblk) * scale2 + (cq - ck)
        if diagonal:
            row = lax.broadcasted_iota(jnp.int32, (tq, tq), 0)
            col = lax.broadcasted_iota(jnp.int32, (tq, tq), 1)
            z = jnp.where(col <= row, z, NEG_INF)
        m_new = jnp.maximum(m, jnp.max(z, axis=-1, keepdims=True))
        alpha = jnp.exp2(m - m_new)
        p = jnp.exp2(z - m_new)
        l_new = alpha * l + jnp.sum(p, axis=-1, keepdims=True)
        acc_new = alpha * acc + _dot(p.astype(BF16), vblk)
        return m_new, l_new, acc_new

    carry = (jnp.full((tq, 1), NEG_INF, F32), jnp.zeros((tq, 1), F32), jnp.zeros((tq, hd), F32))
    carry = lax.fori_loop(0, qb // 2, lambda i, c: step(2 * i + 1, step(2 * i, c, False), False), carry)
    carry = lax.fori_loop((qb // 2) * 2, qb, lambda kb, c: step(kb, c, False), carry)
    _, l, acc = step(qb, carry, True)
    o_ref[...] = (acc / l).astype(o_ref.dtype)


def fox_prompt(qkv16, cum, ck, o_shape, *, heads, hd, seq, tq):
    blk = 2 * seq * hd * 2 + 2 * tq * hd * 2 + tq * V7X_LANES * 4 + 8 * seq * 4
    return pl.pallas_call(
        functools.partial(_fox_prompt_kernel, tq=tq, scale2=hd ** -0.5 * LOG2E),
        grid=(heads, seq // tq),
        in_specs=[pl.BlockSpec((tq, hd), lambda h, qb: (qb, h)),
                  pl.BlockSpec((seq, hd), lambda h, qb: (0, heads + h)),
                  pl.BlockSpec((seq, hd), lambda h, qb: (0, 2 * heads + h)),
                  pl.BlockSpec((None, tq, V7X_LANES), lambda h, qb: (0, qb, 0)),
                  pl.BlockSpec((None, seq // tq, 1, tq), lambda h, qb: (h, 0, 0, 0))],
        out_specs=pl.BlockSpec((tq, hd), lambda h, qb: (qb, h)),
        out_shape=o_shape,
        compiler_params=pltpu.CompilerParams(dimension_semantics=("arbitrary", "arbitrary"),
                                             vmem_limit_bytes=int(min(2 * blk + 32 * 1024 * 1024,
                                                                      V7X_VMEM_BYTES - 2 * 1024 * 1024))),
        name="fox_prompt",
    )(qkv16, qkv16, qkv16, cum, ck)


def _fox_sample_kernel(q_ref, kc_ref, vc_ref, kn_ref, vn_ref, cum_ref, ck_ref, prev_ref, o_ref, *,
                       heads_per_step, hd, past, scale2):
    del prev_ref
    hg = pl.program_id(0)
    t = q_ref.shape[0]
    row = lax.broadcasted_iota(jnp.int32, (t, past + t), 0)
    col = lax.broadcasted_iota(jnp.int32, (t, past + t), 1)
    causal = col <= row + past
    logits = []
    for hh in range(heads_per_step):
        sl = slice(hh * hd, (hh + 1) * hd)
        qh = q_ref[:, sl]
        cq = _head_column(cum_ref[...], hg * heads_per_step + hh) * LOG2E
        ck = ck_ref[hh] * LOG2E
        s = jnp.concatenate([_dot_nt(qh, kc_ref[:, hh, :].astype(BF16)), _dot_nt(qh, kn_ref[:, sl])], axis=1)
        logits.append(jnp.where(causal, s * scale2 + (cq - ck), NEG_INF))
    for hh in range(heads_per_step):
        sl = slice(hh * hd, (hh + 1) * hd)
        z = logits[hh]
        m = jnp.max(z, axis=-1, keepdims=True)
        e = jnp.exp2(z - m)
        p = (e / jnp.sum(e, axis=-1, keepdims=True)).astype(BF16)
        o = _dot(p[:, :past], vc_ref[:, hh, :].astype(BF16)) + _dot(p[:, past:], vn_ref[:, sl])
        o_ref[:, sl] = o.astype(o_ref.dtype)


def fox_sample(qkv16_s, cache_k, cache_v, cum, ck, o_prev, row0_blocks, *, n_seq, heads, hd, t_dec, past):
    hps = V7X_SUBLANES
    assert heads % hps == 0
    width = hps * hd
    n_hg = heads // hps
    ls = past + t_dec
    blk = (2 * past * width * 4 + 3 * t_dec * width * 2 + t_dec * V7X_LANES * 4 + hps * 8 * ls * 4
           + 2 * past * width * 2 + 6 * t_dec * ls * 4)
    return pl.pallas_call(
        functools.partial(_fox_sample_kernel, heads_per_step=hps, hd=hd, past=past, scale2=hd ** -0.5 * LOG2E),
        grid=(n_hg, n_seq),
        in_specs=[pl.BlockSpec((t_dec, width), lambda hg, b: (b, hg)),
                  pl.BlockSpec((past, hps, hd), lambda hg, b: (b, hg, 0)),
                  pl.BlockSpec((past, hps, hd), lambda hg, b: (b, hg, 0)),
                  pl.BlockSpec((t_dec, width), lambda hg, b: (b, n_hg + hg)),
                  pl.BlockSpec((t_dec, width), lambda hg, b: (b, 2 * n_hg + hg)),
                  pl.BlockSpec((None, t_dec, V7X_LANES), lambda hg, b: (b, past // t_dec, 0)),
                  pl.BlockSpec((None, hps, 1, ls), lambda hg, b: (b, hg, 0, 0)),
                  pl.BlockSpec(memory_space=pl.ANY)],
        out_specs=pl.BlockSpec((t_dec, width), lambda hg, b: (row0_blocks + b, hg)),
        out_shape=jax.ShapeDtypeStruct(o_prev.shape, o_prev.dtype),
        input_output_aliases={7: 0},
        compiler_params=_params(("arbitrary", "arbitrary"), blk),
        name="fox_sample",
    )(qkv16_s, cache_k, cache_v, qkv16_s, qkv16_s, cum, ck, o_prev)


def _toeplitz(vals, tq, nk):
    w = tq + nk
    flat = jnp.tile(vals, (1, tq))[:, :tq * w]
    return flat.reshape(vals.shape[0], tq, w)[:, :, :nk]


def _band_bias(rel_to_index, table, q0, k0, tq, nk, n_prev):
    d = jnp.arange(tq + nk + 1)
    j_minus_i = jnp.where(d < nk, d, d - (tq + nk + 1))
    vals = table.astype(F32)[rel_to_index((q0 - k0) - j_minus_i)].T * LOG2E
    qc = (q0 + jnp.arange(tq))[:, None] // CHUNK
    kc = (k0 + jnp.arange(nk))[None, :] // CHUNK
    valid = (kc <= qc) & (kc >= qc - n_prev)
    return jnp.where(valid[None], _toeplitz(vals, tq, nk), NEG_INF)


def _clip_index(clip, rel):
    return jnp.clip(rel, -clip, clip) + clip


def _t5_bucket(n_buckets, rel):
    half = n_buckets // 2
    max_exact = half // 2
    n = jnp.abs(rel)
    ratio = jnp.maximum(n, max_exact).astype(F32) / max_exact
    large = max_exact + (jnp.log(ratio) / math.log(T5_MAX_DIST / max_exact) * (half - max_exact)).astype(jnp.int32)
    large = jnp.minimum(large, half - 1)
    return jnp.where(rel < 0, half, 0) + jnp.where(n < max_exact, n, large)


def _band_mixer(qkv16_p, qkv16_s, m_total, n_seq, t_dec, caches_k, caches_v, layer, rel_to_index, table, sinks, *,
                d_q, kv_heads_total, group, hd, n_prev, past_len, kv_per_step, tq):
    n_prompt = qkv16_p.shape[0]
    kvw = kv_heads_total * hd
    qw_step = kv_per_step * group * hd
    kw_step = kv_per_step * hd
    n_hg = kv_heads_total // kv_per_step
    heads_step = kv_per_step * group
    win = n_prev * CHUNK
    tkb = math.gcd(tq, win)
    n_kblk = (tq + win) // tkb
    n_back = win // tkb
    ratio = tq // tkb
    k_col0 = d_q // kw_step
    v_col0 = (d_q + kvw) // kw_step
    o_shape = jax.ShapeDtypeStruct((m_total, d_q), BF16)

    bias_p = _band_bias(rel_to_index, table, win, 0, tq, tq + win, n_prev)

    def first_blk(qi, t):
        return qi * ratio - n_back + t

    def kmap(t, col0):
        return lambda hg, qi: (jnp.maximum(qi * ratio - n_back + t, 0), col0 + hg)

    k_specs = [pl.BlockSpec((tkb, kw_step), kmap(t, k_col0)) for t in range(n_kblk)]
    v_specs = [pl.BlockSpec((tkb, kw_step), kmap(t, v_col0)) for t in range(n_kblk)]
    blk = (tq * qw_step * 2 * 2 + 2 * n_kblk * tkb * kw_step * 2 + heads_step * tq * (tq + win) * 4
           + 8 * tq * (tq + win) * 4)
    o = band_attention(
        qkv16_p, pl.BlockSpec((tq, qw_step), lambda hg, qi: (qi, hg)),
        [qkv16_p] * n_kblk, k_specs, [qkv16_p] * n_kblk, v_specs, bias_p, sinks, None, o_shape,
        pl.BlockSpec((tq, qw_step), lambda hg, qi: (qi, hg)),
        (n_hg, n_prompt // tq), kv_heads=kv_per_step, group=group, hd=hd, first_blk=first_blk, block_bytes=blk)

    w = caches_k.shape[2]
    bias_s = _band_bias(rel_to_index, table, past_len, past_len - w, t_dec, w + t_dec, n_prev)
    row0 = n_prompt // t_dec
    if hd == V7X_LANES and kv_heads_total % V7X_SUBLANES == 0:
        kv_per_step = V7X_SUBLANES
        qw_step, kw_step, heads_step = kv_per_step * group * hd, kv_per_step * hd, kv_per_step * group
        n_hg, k_col0, v_col0 = kv_heads_total // kv_per_step, d_q // kw_step, (d_q + kvw) // kw_step
        ck2 = caches_k.reshape(-1, kv_heads_total, hd)
        cv2 = caches_v.reshape(-1, kv_heads_total, hd)
        cache_spec = pl.BlockSpec((w, kv_per_step, hd), lambda hg, b: (layer * n_seq + b, hg, 0))
    else:
        ck2 = caches_k[layer].reshape(n_seq * w, kvw)
        cv2 = caches_v[layer].reshape(n_seq * w, kvw)
        cache_spec = pl.BlockSpec((w, kw_step), lambda hg, b: (b, hg))
    k_specs = [cache_spec, pl.BlockSpec((t_dec, kw_step), lambda hg, b: (b, k_col0 + hg))]
    v_specs = [cache_spec, pl.BlockSpec((t_dec, kw_step), lambda hg, b: (b, v_col0 + hg))]
    blk = (t_dec * qw_step * 4 + 2 * w * kw_step * 4 + 2 * t_dec * kw_step * 2
           + heads_step * t_dec * (w + t_dec) * 4 + 8 * t_dec * (w + t_dec) * 4 + 2 * w * kw_step * 2)
    o = band_attention(
        qkv16_s, pl.BlockSpec((t_dec, qw_step), lambda hg, b: (b, hg)),
        [ck2, qkv16_s], k_specs, [cv2, qkv16_s], v_specs, bias_s, sinks, o, o_shape,
        pl.BlockSpec((t_dec, qw_step), lambda hg, b: (row0 + b, hg)),
        (n_hg, n_seq), kv_heads=kv_per_step, group=group, hd=hd, first_blk=None, block_bytes=blk)
    return o


def _roll_kernel(ck_ref, cv_ref, nk_ref, nv_ref, ok_ref, ov_ref):
    last = pl.num_programs(1) - 1

    @pl.when(pl.program_id(1) < last)
    def _():
        ok_ref[...] = ck_ref[...]
        ov_ref[...] = cv_ref[...]

    @pl.when(pl.program_id(1) == last)
    def _():
        ok_ref[...] = nk_ref[...]
        ov_ref[...] = nv_ref[...]


def roll_caches(caches_k, caches_v, new_k, new_v, t_dec):
    shape = caches_k.shape
    n_win, w, minor = shape[0] * shape[1], shape[2], shape[3:]
    assert w % t_dec == 0
    nb = w // t_dec
    flat = (n_win * w, *minor)
    new_shape = (n_win * t_dec, *minor)
    blk = (t_dec, *minor)
    zeros = (0,) * len(minor)
    old_spec = pl.BlockSpec(blk, lambda s, r: (s * nb + jnp.minimum(r + 1, nb - 1), *zeros))
    new_spec = pl.BlockSpec(blk, lambda s, r: (s, *zeros))
    out_spec = pl.BlockSpec(blk, lambda s, r: (s * nb + r, *zeros))
    block_bytes = 6 * t_dec * math.prod(minor[:-2]) * max(minor[-2], V7X_SUBLANES) * max(minor[-1], V7X_LANES) * 4
    ok, ov = pl.pallas_call(
        _roll_kernel,
        grid=(n_win, nb),
        in_specs=[old_spec, old_spec, new_spec, new_spec],
        out_specs=[out_spec, out_spec],
        out_shape=[jax.ShapeDtypeStruct(flat, caches_k.dtype), jax.ShapeDtypeStruct(flat, caches_v.dtype)],
        compiler_params=_params(("arbitrary", "arbitrary"), block_bytes),
        name="roll_caches",
    )(caches_k.reshape(flat), caches_v.reshape(flat),
      jnp.stack(new_k).reshape(new_shape), jnp.stack(new_v).reshape(new_shape))
    return ok.reshape(shape), ov.reshape(shape)


def _qk_gains(qk_g, n_q_heads, n_k_heads, v_width):
    g = jnp.concatenate([jnp.tile(qk_g[0], n_q_heads), jnp.tile(qk_g[1], n_k_heads), jnp.ones((v_width,), F32)])
    return g.reshape(1, -1).astype(F32)


def kernel(x_prompt, x_sample, cache_a_k, cache_a_v, cache_b_k, cache_b_v, cache_b_logf, cache_c_k, cache_c_v,
           norm_g, ffn_w_in, ffn_w_out, a_w_qkv, a_w_o, a_qk_g, a_rel_bias, b_w_qkvf, b_f_bias, b_w_o, b_qk_g,
           c_w_qkv, c_w_o, c_qk_g, c_sinks, t5_bias):
    batch, seq, d = x_prompt.shape
    n_seq, t_dec, _ = x_sample.shape
    assert batch == 1 and t_dec == CHUNK and seq % 256 == 0
    depth = norm_g.shape[0]
    past_len = cache_b_k.shape[2]
    assert past_len % CHUNK == 0
    n_prompt = batch * seq
    n_samp = n_seq * t_dec
    m = n_prompt + n_samp

    h_a, hd_a = a_rel_bias.shape[2], a_qk_g.shape[2]
    h_b, hd_b = b_f_bias.shape[1], b_qk_g.shape[2]
    h_c, hd_c = c_sinks.shape[1], c_qk_g.shape[2]
    kv_c = cache_c_k.shape[3]
    g_c = h_c // kv_c
    nb_c = cache_c_k.shape[2] // CHUNK
    assert hd_a == V7X_LANES and hd_b == V7X_LANES

    x = jnp.concatenate([x_prompt.reshape(n_prompt, d), x_sample.reshape(n_samp, d)], axis=0)

    ffn_w_out16 = ffn_w_out.astype(BF16)

    def half_ffn(x, g, layer, half, split_rows=False):
        xn = rms_norm_rows(x, g)
        h = ffn_in(xn, ffn_w_in, (layer, half))
        if not split_rows:
            return matmul_residual(h, ffn_w_out16, (layer, half), x, FFN_RES, 512)
        return (matmul_residual(h, ffn_w_out16, (layer, half), x, FFN_RES, 512, 0, n_prompt),
                matmul_residual(h, ffn_w_out16, (layer, half), x, FFN_RES, 512, n_prompt, n_samp))

    a_kp, a_vp, a_ks, a_vs = [], [], [], []
    b_kp, b_vp, b_lp, b_ks, b_vs, b_ls = [], [], [], [], [], []
    c_kp, c_vp, c_ks, c_vs = [], [], [], []

    for i in range(depth):
        kind, j = i % 3, i // 3
        x = half_ffn(x, norm_g[i, 0], i, 0)
        u = rms_norm_rows(x, norm_g[i, 1])
        if kind == 0:
            dq = h_a * hd_a
            gains = _qk_gains(a_qk_g[j], h_a, h_a, dq)
            qkv_p, kv_p = qkv_project(u, 0, n_prompt, a_w_qkv, (j,), gains, hd_a, dq, dq)
            qkv_s, kv_s = qkv_project(u, n_prompt, n_samp, a_w_qkv, (j,), gains, hd_a, dq, dq)
            k_p, v_p, k_s, v_s = kv_p[:, :dq], kv_p[:, dq:], kv_s[:, :dq], kv_s[:, dq:]
            clip = (a_rel_bias.shape[1] - 1) // 2
            o = _band_mixer(qkv_p, qkv_s, m, n_seq, t_dec, cache_a_k, cache_a_v, j,
                            functools.partial(_clip_index, clip), a_rel_bias[j], None,
                            d_q=dq, kv_heads_total=h_a, group=1, hd=hd_a, n_prev=NB_A, past_len=past_len,
                            kv_per_step=8 if h_a % 8 == 0 else 4, tq=256)
            x = matmul_residual(o, a_w_o, (j,), x, 1.0, 1024)
            w = min(NB_A * CHUNK, seq)
            a_kp.append(k_p[n_prompt - w:].reshape(batch, w, h_a, hd_a))
            a_vp.append(v_p[n_prompt - w:].reshape(batch, w, h_a, hd_a))
            a_ks.append(k_s)
            a_vs.append(v_s)
        elif kind == 1:
            dq = h_b * hd_b
            gains = _qk_gains(b_qk_g[j], h_b, h_b, dq)
            qkv_p, kv_p = qkv_project(u, 0, n_prompt, b_w_qkvf, (j,), gains, hd_b, dq, dq)
            qkv_s, kv_s = qkv_project(u, n_prompt, n_samp, b_w_qkvf, (j,), gains, hd_b, dq, dq)
            k_p, v_p, k_s, v_s = kv_p[:, :dq], kv_p[:, dq:], kv_s[:, :dq], kv_s[:, dq:]
            w_f = jnp.pad(b_w_qkvf[j, :, 3 * dq:].astype(BF16), ((0, 0), (0, V7X_LANES - h_b)))
            b_f = jnp.pad(b_f_bias[j].astype(F32), (0, V7X_LANES - h_b)).reshape(1, V7X_LANES)
            logf = forget_gate(u, w_f, b_f)
            tq = 1024 if seq % 1024 == 0 else 256
            cum_p = cumsum_time(logf, batch, seq, 256)
            ck = cum_p[0, :, :h_b].T.reshape(h_b, seq // tq, 1, tq)
            o = fox_prompt(qkv_p, cum_p, ck, jax.ShapeDtypeStruct((m, dq), BF16),
                           heads=h_b, hd=hd_b, seq=seq, tq=tq)
            ls = past_len + t_dec
            logf_all = jnp.concatenate(
                [jnp.pad(cache_b_logf[j].astype(F32), ((0, 0), (0, 0), (0, V7X_LANES - h_b))),
                 logf[n_prompt:].reshape(n_seq, t_dec, V7X_LANES)], axis=1)
            cum_s = cumsum_time(logf_all.reshape(n_seq * ls, V7X_LANES), n_seq, ls, CHUNK)
            ck = cum_s[:, :, :h_b].transpose(0, 2, 1)[:, :, None, :]
            o = fox_sample(qkv_s, cache_b_k[j].reshape(n_seq * past_len, h_b, hd_b),
                           cache_b_v[j].reshape(n_seq * past_len, h_b, hd_b), cum_s, ck, o, n_prompt // t_dec,
                           n_seq=n_seq, heads=h_b, hd=hd_b, t_dec=t_dec, past=past_len)
            x = matmul_residual(o, b_w_o, (j,), x, 1.0, 1024)
            b_kp.append(k_p.reshape(batch, seq, h_b, hd_b))
            b_vp.append(v_p.reshape(batch, seq, h_b, hd_b))
            b_lp.append(logf[:n_prompt, :h_b].reshape(batch, seq, h_b))
            b_ks.append(k_s.reshape(n_seq, t_dec, h_b, hd_b))
            b_vs.append(v_s.reshape(n_seq, t_dec, h_b, hd_b))
            b_ls.append(logf[n_prompt:, :h_b].reshape(n_seq, t_dec, h_b))
        else:
            dq, kvw = h_c * hd_c, kv_c * hd_c
            gains = _qk_gains(c_qk_g[j], h_c, kv_c, kvw)
            qkv_p, kv_p = qkv_project(u, 0, n_prompt, c_w_qkv, (j,), gains, hd_c, dq, kvw)
            qkv_s, kv_s = qkv_project(u, n_prompt, n_samp, c_w_qkv, (j,), gains, hd_c, dq, kvw)
            k_p, v_p, k_s, v_s = kv_p[:, :kvw], kv_p[:, kvw:], kv_s[:, :kvw], kv_s[:, kvw:]
            o = _band_mixer(qkv_p, qkv_s, m, n_seq, t_dec, cache_c_k, cache_c_v, j,
                            functools.partial(_t5_bucket, t5_bias.shape[0]), t5_bias, c_sinks[j].astype(F32),
                            d_q=dq, kv_heads_total=kv_c, group=g_c, hd=hd_c, n_prev=nb_c, past_len=past_len,
                            kv_per_step=2, tq=256)
            x = matmul_residual(o, c_w_o, (j,), x, 1.0, 1024)
            w = min(nb_c * CHUNK, seq)
            c_kp.append(k_p[n_prompt - w:].reshape(batch, w, kv_c, hd_c))
            c_vp.append(v_p[n_prompt - w:].reshape(batch, w, kv_c, hd_c))
            c_ks.append(k_s)
            c_vs.append(v_s)
        x = half_ffn(x, norm_g[i, 2], i, 1, split_rows=(i == depth - 1))

    y_prompt = x[0].reshape(batch, seq, d)
    y_sample = x[1].reshape(n_seq, t_dec, d)
    a_k_sample, a_v_sample = roll_caches(cache_a_k, cache_a_v, a_ks, a_vs, t_dec)
    c_k_sample, c_v_sample = roll_caches(cache_c_k, cache_c_v, c_ks, c_vs, t_dec)
    return (y_prompt, y_sample,
            jnp.stack(a_kp), jnp.stack(a_vp), a_k_sample, a_v_sample,
            jnp.stack(b_kp), jnp.stack(b_vp), jnp.stack(b_lp), jnp.stack(b_ks), jnp.stack(b_vs), jnp.stack(b_ls),
            jnp.stack(c_kp), jnp.stack(c_vp), c_k_sample, c_v_sample)
```

```python
import functools
import math

import jax
import jax.numpy as jnp
from jax import lax
from jax.experimental import pallas as pl
from jax.experimental.pallas import tpu as pltpu

CHUNK = 64
NB_A = 8
EPS = 1e-6
NEG_INF = -1e30
FFN_RES = 0.5
T5_MAX_DIST = 128
LOG2E = math.log2(math.e)

V7X_LANES = 128
V7X_SUBLANES = 8
V7X_VMEM_BYTES = 64 * 1024 * 1024
VMEM_HEADROOM_BYTES = 14 * 1024 * 1024

F32 = jnp.float32
BF16 = jnp.bfloat16


def _params(semantics, block_bytes):
    limit = min(2 * block_bytes + VMEM_HEADROOM_BYTES, V7X_VMEM_BYTES - 2 * 1024 * 1024)
    return pltpu.CompilerParams(dimension_semantics=semantics, vmem_limit_bytes=int(limit))


def _row_tile(m, want):
    t = want
    while m % t:
        t //= 2
    return t


def _dot(a, b):
    return jnp.dot(a, b, preferred_element_type=F32)


def _dot_nt(a, b):
    return lax.dot_general(a, b, (((1,), (1,)), ((), ())), preferred_element_type=F32)


def _rms_kernel(x_ref, g_ref, o_ref):
    x = x_ref[...]
    ms = jnp.mean(x * x, axis=-1, keepdims=True)
    o_ref[...] = ((x * lax.rsqrt(ms + EPS)) * g_ref[...]).astype(o_ref.dtype)


def rms_norm_rows(x, g):
    m, d = x.shape
    tm = _row_tile(m, 256)
    return pl.pallas_call(
        _rms_kernel,
        grid=(m // tm,),
        in_specs=[pl.BlockSpec((tm, d), lambda i: (i, 0)),
                  pl.BlockSpec((1, d), lambda i: (0, 0))],
        out_specs=pl.BlockSpec((tm, d), lambda i: (i, 0)),
        out_shape=jax.ShapeDtypeStruct((m, d), BF16),
        compiler_params=_params(("arbitrary",), tm * d * 6),
        name="rms_norm",
    )(x, g.reshape(1, d).astype(F32))


def _ffn_in_kernel(x_ref, wg_ref, wu_ref, o_ref):
    x = x_ref[...]
    gate = _dot(x, wg_ref[...].astype(BF16))
    up = _dot(x, wu_ref[...].astype(BF16))
    o_ref[...] = (gate * jax.nn.sigmoid(gate) * up).astype(o_ref.dtype)


def _ffn_tile(f):
    for tf in (256, 128):
        if f % tf == 0:
            return tf
    raise ValueError(f"D_FF={f} must be a multiple of {V7X_LANES}")


def _weight_spec(lead, k, tn, col):
    return pl.BlockSpec((None,) * len(lead) + (k, tn), lambda i, j: (*lead, 0, col(j)))


def ffn_in(xn, w_in, lead):
    m, k = xn.shape
    f = w_in.shape[-1] // 2
    tf = _ffn_tile(f)
    n_f = f // tf
    tm = _row_tile(m, 1024)
    blk = tm * k * 2 + 2 * k * tf * (4 + 1) + tm * tf * 2 + tm * 2 * tf * 4
    return pl.pallas_call(
        _ffn_in_kernel,
        grid=(m // tm, n_f),
        in_specs=[pl.BlockSpec((tm, k), lambda i, j: (i, 0)),
                  _weight_spec(lead, k, tf, lambda j: j),
                  _weight_spec(lead, k, tf, lambda j: n_f + j)],
        out_specs=pl.BlockSpec((tm, tf), lambda i, j: (i, j)),
        out_shape=jax.ShapeDtypeStruct((m, f), BF16),
        compiler_params=_params(("arbitrary", "arbitrary"), blk),
        name="ffn_in",
    )(xn, w_in, w_in)


def _mm_res_kernel(a_ref, w_ref, r_ref, o_ref, *, scale):
    acc = _dot(a_ref[...], w_ref[...].astype(BF16))
    o_ref[...] = r_ref[...] + scale * acc


def matmul_residual(a, w, lead, res, scale, tm_want, row0=0, rows=None):
    m, k = a.shape
    n = w.shape[-1]
    rows = m if rows is None else rows
    tm = _row_tile(math.gcd(row0, rows) if row0 else rows, tm_want)
    tn = _row_tile(n, 512)
    i0 = row0 // tm
    blk = tm * k * 2 + k * tn * (w.dtype.itemsize + 1) + 3 * tm * tn * 4
    return pl.pallas_call(
        functools.partial(_mm_res_kernel, scale=scale),
        grid=(rows // tm, n // tn),
        in_specs=[pl.BlockSpec((tm, k), lambda i, j: (i0 + i, 0)),
                  _weight_spec(lead, k, tn, lambda j: j),
                  pl.BlockSpec((tm, tn), lambda i, j: (i0 + i, j))],
        out_specs=pl.BlockSpec((tm, tn), lambda i, j: (i, j)),
        out_shape=jax.ShapeDtypeStruct((rows, n), F32),
        compiler_params=_params(("arbitrary", "arbitrary"), blk),
        name="matmul_residual",
    )(a, w, res)


def _head_rms(acc, hd, normalise):
    tn = acc.shape[1]
    parts = []
    if hd == V7X_LANES:
        for h in range(tn // hd):
            xh = acc[:, h * hd:(h + 1) * hd]
            ms = jnp.mean(xh * xh, axis=-1, keepdims=True)
            parts.append(xh * jnp.where(normalise, lax.rsqrt(ms + EPS), 1.0))
    else:
        assert hd * 2 == V7X_LANES
        lo = lax.broadcasted_iota(jnp.int32, (1, V7X_LANES), 1) < hd
        for h in range(tn // V7X_LANES):
            xh = acc[:, h * V7X_LANES:(h + 1) * V7X_LANES]
            x2 = xh * xh
            s_lo = jnp.sum(jnp.where(lo, x2, 0.0), axis=-1, keepdims=True)
            s_hi = jnp.sum(jnp.where(lo, 0.0, x2), axis=-1, keepdims=True)
            ms = jnp.where(lo, s_lo, s_hi) * (1.0 / hd)
            parts.append(xh * jnp.where(normalise, lax.rsqrt(ms + EPS), 1.0))
    return jnp.concatenate(parts, axis=-1)


def _qkv_kernel(x_ref, w_ref, g_ref, o16_ref, o32_ref, *, hd, n_norm_tiles, row_chunks):
    normalise = pl.program_id(1) < n_norm_tiles
    g = g_ref[...]
    w = w_ref[...].astype(BF16)
    rc = x_ref.shape[0] // row_chunks
    for c in range(row_chunks):
        rows = slice(c * rc, (c + 1) * rc)
        y = _head_rms(_dot(x_ref[rows, :], w), hd, normalise) * g
        o16_ref[rows, :] = y.astype(o16_ref.dtype)
        o32_ref[rows, :] = y


def qkv_project(xn, row0, rows, w, lead, gains, hd, q_width, kv_width):
    k = xn.shape[1]
    n = q_width + 2 * kv_width
    tm = _row_tile(math.gcd(row0, rows) if row0 else rows, 1024)
    tn = _row_tile(math.gcd(q_width, kv_width), 512)
    n_q, n_k = q_width // tn, kv_width // tn
    i0 = row0 // tm
    row_chunks = 4 if tm % 64 == 0 else 1
    blk = tm * k * 2 + k * tn * (w.dtype.itemsize + 1) + tm * tn * (2 + 4 + 4)
    return pl.pallas_call(
        functools.partial(_qkv_kernel, hd=hd, n_norm_tiles=n_q + n_k, row_chunks=row_chunks),
        grid=(rows // tm, n // tn),
        in_specs=[pl.BlockSpec((tm, k), lambda i, j: (i0 + i, 0)),
                  _weight_spec(lead, k, tn, lambda j: j),
                  pl.BlockSpec((1, tn), lambda i, j: (0, j))],
        out_specs=[pl.BlockSpec((tm, tn), lambda i, j: (i, j)),
                   pl.BlockSpec((tm, tn), lambda i, j: (i, jnp.maximum(j - n_q, 0)))],
        out_shape=[jax.ShapeDtypeStruct((rows, n), BF16),
                   jax.ShapeDtypeStruct((rows, 2 * kv_width), F32)],
        compiler_params=_params(("arbitrary", "arbitrary"), blk),
        name="qkv_project",
    )(xn, w, gains)


def _fgate_kernel(x_ref, w_ref, b_ref, o_ref):
    z = _dot(x_ref[...], w_ref[...]) + b_ref[...]
    o_ref[...] = jnp.minimum(z, 0.0) - jnp.log1p(jnp.exp(-jnp.abs(z)))


def forget_gate(xn, w_pad, b_pad):
    m, k = xn.shape
    n = w_pad.shape[1]
    tm = _row_tile(m, 1024)
    blk = tm * k * 2 + k * n * 2 + tm * n * 8
    return pl.pallas_call(
        _fgate_kernel,
        grid=(m // tm,),
        in_specs=[pl.BlockSpec((tm, k), lambda i: (i, 0)),
                  pl.BlockSpec((k, n), lambda i: (0, 0)),
                  pl.BlockSpec((1, n), lambda i: (0, 0))],
        out_specs=pl.BlockSpec((tm, n), lambda i: (i, 0)),
        out_shape=jax.ShapeDtypeStruct((m, n), F32),
        compiler_params=_params(("arbitrary",), blk),
        name="forget_gate",
    )(xn, w_pad, b_pad)


def _cumsum_kernel(x_ref, o_ref, *, blk):
    length, width = x_ref.shape
    row = lax.broadcasted_iota(jnp.int32, (blk, blk), 0)
    col = lax.broadcasted_iota(jnp.int32, (blk, blk), 1)
    tri = (row >= col).astype(F32)

    def body(i, carry):
        start = pl.multiple_of(i * blk, blk)
        xs = x_ref[pl.ds(start, blk), :]
        c = jnp.dot(tri, xs, precision=lax.Precision.HIGHEST, preferred_element_type=F32) + carry
        o_ref[pl.ds(start, blk), :] = c
        return c[blk - 1:blk, :]

    lax.fori_loop(0, length // blk, body, jnp.zeros((1, width), F32))


def cumsum_time(x, n_seq, length, blk):
    width = x.shape[1]
    return pl.pallas_call(
        functools.partial(_cumsum_kernel, blk=blk),
        grid=(n_seq,),
        in_specs=[pl.BlockSpec((length, width), lambda i: (i, 0))],
        out_specs=pl.BlockSpec((None, length, width), lambda i: (i, 0, 0)),
        out_shape=jax.ShapeDtypeStruct((n_seq, length, width), F32),
        compiler_params=_params(("arbitrary",), 2 * length * width * 4),
        name="cumsum_time",
    )(x)


def _head_major(ref):
    return jnp.swapaxes(ref[...], 0, 1) if len(ref.shape) == 3 else ref


def _head_rows(blk, head, hd):
    if len(blk.shape) == 3:
        return blk[head]
    return blk[:, head * hd:(head + 1) * hd]


def _band_kernel(*refs, n_kblk, kv_heads, group, hd, scale2, has_sink, first_blk, heads_per_step):
    q_ref = refs[0]
    k_refs = refs[1:1 + n_kblk]
    v_refs = refs[1 + n_kblk:1 + 2 * n_kblk]
    bias_ref = refs[1 + 2 * n_kblk]
    sink_ref = refs[2 + 2 * n_kblk] if has_sink else None
    o_ref = refs[-1]

    hg = pl.program_id(0)
    qi = pl.program_id(1)
    widths = [r.shape[0] for r in k_refs]
    k_refs = [_head_major(r) for r in k_refs]
    v_refs = [_head_major(r) for r in v_refs]
    logits = []
    for kv in range(kv_heads):
        kparts = [_head_rows(r, kv, hd).astype(BF16) for r in k_refs]
        for g in range(group):
            h = kv * group + g
            qh = q_ref[:, h * hd:(h + 1) * hd]
            sparts = []
            for t, kp in enumerate(kparts):
                st = _dot_nt(qh, kp)
                if first_blk is not None:
                    st = jnp.where(first_blk(qi, t) >= 0, st, NEG_INF)
                sparts.append(st)
            s = jnp.concatenate(sparts, axis=1) if len(sparts) > 1 else sparts[0]
            logits.append(s * scale2 + bias_ref[h])
    outs = []
    for kv in range(kv_heads):
        vparts = [_head_rows(r, kv, hd).astype(BF16) for r in v_refs]
        for g in range(group):
            h = kv * group + g
            z = logits[h]
            m = jnp.max(z, axis=-1, keepdims=True)
            if has_sink:
                sk = sink_ref[hg * heads_per_step + h] * LOG2E
                m = jnp.maximum(m, sk)
            e = jnp.exp2(z - m)
            denom = jnp.sum(e, axis=-1, keepdims=True)
            if has_sink:
                denom = denom + jnp.exp2(sk - m)
            p = (e / denom).astype(BF16)
            off = 0
            acc = None
            for w_t, vp in zip(widths, vparts):
                part = _dot(p[:, off:off + w_t], vp)
                acc = part if acc is None else acc + part
                off += w_t
            outs.append(acc)
    per_store = V7X_LANES // hd
    for c in range(len(outs) // per_store):
        piece = outs[c * per_store:(c + 1) * per_store]
        val = jnp.concatenate(piece, axis=1) if per_store > 1 else piece[0]
        o_ref[:, c * V7X_LANES:(c + 1) * V7X_LANES] = val.astype(o_ref.dtype)


def _drop_alias_ref(kern):
    def body(*refs):
        return kern(*refs[:-2], refs[-1])
    return body


def band_attention(q_arr, q_spec, k_arrs, k_specs, v_arrs, v_specs, bias, sinks, o_prev, o_shape, o_spec,
                   grid, *, kv_heads, group, hd, first_blk, block_bytes):
    heads_per_step = kv_heads * group
    n_kblk = len(k_arrs)
    has_sink = sinks is not None
    tq, nk = bias.shape[1], bias.shape[2]
    in_specs = [q_spec, *k_specs, *v_specs, pl.BlockSpec((heads_per_step, tq, nk), lambda a, b: (a, 0, 0))]
    args = [q_arr, *k_arrs, *v_arrs, bias]
    if has_sink:
        in_specs.append(pl.BlockSpec(memory_space=pltpu.SMEM))
        args.append(sinks)
    body = functools.partial(_band_kernel, n_kblk=n_kblk, kv_heads=kv_heads, group=group, hd=hd,
                             scale2=hd ** -0.5 * LOG2E, has_sink=has_sink, first_blk=first_blk,
                             heads_per_step=heads_per_step)
    aliases = {}
    if o_prev is not None:
        in_specs.append(pl.BlockSpec(memory_space=pl.ANY))
        args.append(o_prev)
        aliases = {len(args) - 1: 0}
        body = _drop_alias_ref(body)
    return pl.pallas_call(
        body,
        grid=grid,
        in_specs=in_specs,
        out_specs=o_spec,
        out_shape=o_shape,
        input_output_aliases=aliases,
        compiler_params=_params(("arbitrary", "arbitrary"), block_bytes),
        name="band_attention",
    )(*args)


def _head_column(cum_blk, h):
    lane = lax.broadcasted_iota(jnp.int32, (1, V7X_LANES), 1)
    return jnp.sum(jnp.where(lane == h, cum_blk, 0.0), axis=-1, keepdims=True)


def _fox_prompt_kernel(q_ref, k_ref, v_ref, cum_ref, ck_ref, o_ref, *, tq, scale2):
    h = pl.program_id(0)
    qb = pl.program_id(1)
    q = q_ref[...]
    hd = q.shape[1]
    cq = _head_column(cum_ref[...], h) * LOG2E

    def step(kb, carry, diagonal):
        m, l, acc = carry
        start = pl.multiple_of(kb * tq, tq)
        kblk = k_ref[pl.ds(start, tq), :]
        vblk = v_ref[pl.ds(start, tq), :]
        ck = ck_ref[kb] * LOG2E
        z = _dot_nt(q, kblk) * scale2 + (cq - ck)
        if diagonal:
            row = lax.broadcasted_iota(jnp.int32, (tq, tq), 0)
            col = lax.broadcasted_iota(jnp.int32, (tq, tq), 1)
            z = jnp.where(col <= row, z, NEG_INF)
        m_new = jnp.maximum(m, jnp.max(z, axis=-1, keepdims=True))
        alpha = jnp.exp2(m - m_new)
        p = jnp.exp2(z - m_new)
        l_new = alpha * l + jnp.sum(p, axis=-1, keepdims=True)
        acc_new = alpha * acc + _dot(p.astype(BF16), vblk)
        return m_new, l_new, acc_new

    carry = (jnp.full((tq, 1), NEG_INF, F32), jnp.zeros((tq, 1), F32), jnp.zeros((tq, hd), F32))
    carry = lax.fori_loop(0, qb // 2, lambda i, c: step(2 * i + 1, step(2 * i, c, False), False), carry)
    carry = lax.fori_loop((qb // 2) * 2, qb, lambda kb, c: step(kb, c, False), carry)
    _, l, acc = step(qb, carry, True)
    o_ref[...] = (acc / l).astype(o_ref.dtype)


def fox_prompt(qkv16, cum, ck, o_shape, *, heads, hd, seq, tq):
    blk = 2 * seq * hd * 2 + 2 * tq * hd * 2 + tq * V7X_LANES * 4 + 8 * seq * 4
    return pl.pallas_call(
        functools.partial(_fox_prompt_kernel, tq=tq, scale2=hd ** -0.5 * LOG2E),
        grid=(heads, seq // tq),
        in_specs=[pl.BlockSpec((tq, hd), lambda h, qb: (qb, h)),
                  pl.BlockSpec((seq, hd), lambda h, qb: (0, heads + h)),
                  pl.BlockSpec((seq, hd), lambda h, qb: (0, 2 * heads + h)),
                  pl.BlockSpec((None, tq, V7X_LANES), lambda h, qb: (0, qb, 0)),
                  pl.BlockSpec((None, seq // tq, 1, tq), lambda h, qb: (h, 0, 0, 0))],
        out_specs=pl.BlockSpec((tq, hd), lambda h, qb: (qb, h)),
        out_shape=o_shape,
        compiler_params=pltpu.CompilerParams(dimension_semantics=("arbitrary", "arbitrary"),
                                             vmem_limit_bytes=int(min(2 * blk + 32 * 1024 * 1024,
                                                                      V7X_VMEM_BYTES - 2 * 1024 * 1024))),
        name="fox_prompt",
    )(qkv16, qkv16, qkv16, cum, ck)


def _fox_sample_kernel(q_ref, kc_ref, vc_ref, kn_ref, vn_ref, cum_ref, ck_ref, prev_ref, o_ref, *,
                       heads_per_step, hd, past, scale2):
    del prev_ref
    hg = pl.program_id(0)
    t = q_ref.shape[0]
    row = lax.broadcasted_iota(jnp.int32, (t, past + t), 0)
    col = lax.broadcasted_iota(jnp.int32, (t, past + t), 1)
    causal = col <= row + past
    kc = _head_major(kc_ref)
    vc = _head_major(vc_ref)
    logits = []
    for hh in range(heads_per_step):
        sl = slice(hh * hd, (hh + 1) * hd)
        qh = q_ref[:, sl]
        cq = _head_column(cum_ref[...], hg * heads_per_step + hh) * LOG2E
        ck = ck_ref[hh] * LOG2E
        s = jnp.concatenate([_dot_nt(qh, kc[hh].astype(BF16)), _dot_nt(qh, kn_ref[:, sl])], axis=1)
        logits.append(jnp.where(causal, s * scale2 + (cq - ck), NEG_INF))
    for hh in range(heads_per_step):
        sl = slice(hh * hd, (hh + 1) * hd)
        z = logits[hh]
        m = jnp.max(z, axis=-1, keepdims=True)
        e = jnp.exp2(z - m)
        p = (e / jnp.sum(e, axis=-1, keepdims=True)).astype(BF16)
        o = _dot(p[:, :past], vc[hh].astype(BF16)) + _dot(p[:, past:], vn_ref[:, sl])
        o_ref[:, sl] = o.astype(o_ref.dtype)


def fox_sample(qkv16_s, cache_k, cache_v, cum, ck, o_prev, row0_blocks, *, n_seq, heads, hd, t_dec, past):
    hps = V7X_SUBLANES
    assert heads % hps == 0
    width = hps * hd
    n_hg = heads // hps
    ls = past + t_dec
    blk = (2 * past * width * 4 + 3 * t_dec * width * 2 + t_dec * V7X_LANES * 4 + hps * 8 * ls * 4
           + 2 * past * width * 2 + 6 * t_dec * ls * 4)
    return pl.pallas_call(
        functools.partial(_fox_sample_kernel, heads_per_step=hps, hd=hd, past=past, scale2=hd ** -0.5 * LOG2E),
        grid=(n_hg, n_seq),
        in_specs=[pl.BlockSpec((t_dec, width), lambda hg, b: (b, hg)),
                  pl.BlockSpec((past, hps, hd), lambda hg, b: (b, hg, 0)),
                  pl.BlockSpec((past, hps, hd), lambda hg, b: (b, hg, 0)),
                  pl.BlockSpec((t_dec, width), lambda hg, b: (b, n_hg + hg)),
                  pl.BlockSpec((t_dec, width), lambda hg, b: (b, 2 * n_hg + hg)),
                  pl.BlockSpec((None, t_dec, V7X_LANES), lambda hg, b: (b, past // t_dec, 0)),
                  pl.BlockSpec((None, hps, 1, ls), lambda hg, b: (b, hg, 0, 0)),
                  pl.BlockSpec(memory_space=pl.ANY)],
        out_specs=pl.BlockSpec((t_dec, width), lambda hg, b: (row0_blocks + b, hg)),
        out_shape=jax.ShapeDtypeStruct(o_prev.shape, o_prev.dtype),
        input_output_aliases={7: 0},
        compiler_params=_params(("arbitrary", "arbitrary"), blk),
        name="fox_sample",
    )(qkv16_s, cache_k, cache_v, qkv16_s, qkv16_s, cum, ck, o_prev)


def _toeplitz(vals, tq, nk):
    w = tq + nk
    flat = jnp.tile(vals, (1, tq))[:, :tq * w]
    return flat.reshape(vals.shape[0], tq, w)[:, :, :nk]


def _band_bias(rel_to_index, table, q0, k0, tq, nk, n_prev):
    d = jnp.arange(tq + nk + 1)
    j_minus_i = jnp.where(d < nk, d, d - (tq + nk + 1))
    vals = table.astype(F32)[rel_to_index((q0 - k0) - j_minus_i)].T * LOG2E
    qc = (q0 + jnp.arange(tq))[:, None] // CHUNK
    kc = (k0 + jnp.arange(nk))[None, :] // CHUNK
    valid = (kc <= qc) & (kc >= qc - n_prev)
    return jnp.where(valid[None], _toeplitz(vals, tq, nk), NEG_INF)


def _clip_index(clip, rel):
    return jnp.clip(rel, -clip, clip) + clip


def _t5_bucket(n_buckets, rel):
    half = n_buckets // 2
    max_exact = half // 2
    n = jnp.abs(rel)
    ratio = jnp.maximum(n, max_exact).astype(F32) / max_exact
    large = max_exact + (jnp.log(ratio) / math.log(T5_MAX_DIST / max_exact) * (half - max_exact)).astype(jnp.int32)
    large = jnp.minimum(large, half - 1)
    return jnp.where(rel < 0, half, 0) + jnp.where(n < max_exact, n, large)


def _band_mixer(qkv16_p, qkv16_s, m_total, n_seq, t_dec, caches_k, caches_v, layer, rel_to_index, table, sinks, *,
                d_q, kv_heads_total, group, hd, n_prev, past_len, kv_per_step, tq):
    n_prompt = qkv16_p.shape[0]
    kvw = kv_heads_total * hd
    qw_step = kv_per_step * group * hd
    kw_step = kv_per_step * hd
    n_hg = kv_heads_total // kv_per_step
    heads_step = kv_per_step * group
    win = n_prev * CHUNK
    tkb = math.gcd(tq, win)
    n_kblk = (tq + win) // tkb
    n_back = win // tkb
    ratio = tq // tkb
    k_col0 = d_q // kw_step
    v_col0 = (d_q + kvw) // kw_step
    o_shape = jax.ShapeDtypeStruct((m_total, d_q), BF16)

    bias_p = _band_bias(rel_to_index, table, win, 0, tq, tq + win, n_prev)

    def first_blk(qi, t):
        return qi * ratio - n_back + t

    def kmap(t, col0):
        return lambda hg, qi: (jnp.maximum(qi * ratio - n_back + t, 0), col0 + hg)

    k_specs = [pl.BlockSpec((tkb, kw_step), kmap(t, k_col0)) for t in range(n_kblk)]
    v_specs = [pl.BlockSpec((tkb, kw_step), kmap(t, v_col0)) for t in range(n_kblk)]
    blk = (tq * qw_step * 2 * 2 + 2 * n_kblk * tkb * kw_step * 2 + heads_step * tq * (tq + win) * 4
           + 8 * tq * (tq + win) * 4)
    o = band_attention(
        qkv16_p, pl.BlockSpec((tq, qw_step), lambda hg, qi: (qi, hg)),
        [qkv16_p] * n_kblk, k_specs, [qkv16_p] * n_kblk, v_specs, bias_p, sinks, None, o_shape,
        pl.BlockSpec((tq, qw_step), lambda hg, qi: (qi, hg)),
        (n_hg, n_prompt // tq), kv_heads=kv_per_step, group=group, hd=hd, first_blk=first_blk, block_bytes=blk)

    w = caches_k.shape[2]
    bias_s = _band_bias(rel_to_index, table, past_len, past_len - w, t_dec, w + t_dec, n_prev)
    row0 = n_prompt // t_dec
    if hd == V7X_LANES and kv_heads_total % V7X_SUBLANES == 0:
        kv_per_step = V7X_SUBLANES
        qw_step, kw_step, heads_step = kv_per_step * group * hd, kv_per_step * hd, kv_per_step * group
        n_hg, k_col0, v_col0 = kv_heads_total // kv_per_step, d_q // kw_step, (d_q + kvw) // kw_step
        ck2 = caches_k.reshape(-1, kv_heads_total, hd)
        cv2 = caches_v.reshape(-1, kv_heads_total, hd)
        cache_spec = pl.BlockSpec((w, kv_per_step, hd), lambda hg, b: (layer * n_seq + b, hg, 0))
    else:
        ck2 = caches_k[layer].reshape(n_seq * w, kvw)
        cv2 = caches_v[layer].reshape(n_seq * w, kvw)
        cache_spec = pl.BlockSpec((w, kw_step), lambda hg, b: (b, hg))
    k_specs = [cache_spec, pl.BlockSpec((t_dec, kw_step), lambda hg, b: (b, k_col0 + hg))]
    v_specs = [cache_spec, pl.BlockSpec((t_dec, kw_step), lambda hg, b: (b, v_col0 + hg))]
    blk = (t_dec * qw_step * 4 + 2 * w * kw_step * 4 + 2 * t_dec * kw_step * 2
           + heads_step * t_dec * (w + t_dec) * 4 + 8 * t_dec * (w + t_dec) * 4 + 2 * w * kw_step * 2)
    o = band_attention(
        qkv16_s, pl.BlockSpec((t_dec, qw_step), lambda hg, b: (b, hg)),
        [ck2, qkv16_s], k_specs, [cv2, qkv16_s], v_specs, bias_s, sinks, o, o_shape,
        pl.BlockSpec((t_dec, qw_step), lambda hg, b: (row0 + b, hg)),
        (n_hg, n_seq), kv_heads=kv_per_step, group=group, hd=hd, first_blk=None, block_bytes=blk)
    return o


def _roll_kernel(ck_ref, cv_ref, nk_ref, nv_ref, ok_ref, ov_ref):
    last = pl.num_programs(1) - 1

    @pl.when(pl.program_id(1) < last)
    def _():
        ok_ref[...] = ck_ref[...]
        ov_ref[...] = cv_ref[...]

    @pl.when(pl.program_id(1) == last)
    def _():
        ok_ref[...] = nk_ref[...]
        ov_ref[...] = nv_ref[...]


def roll_caches(caches_k, caches_v, new_k, new_v, t_dec):
    shape = caches_k.shape
    n_win, w, minor = shape[0] * shape[1], shape[2], shape[3:]
    assert w % t_dec == 0
    nb = w // t_dec
    flat = (n_win * w, *minor)
    new_shape = (n_win * t_dec, *minor)
    blk = (t_dec, *minor)
    zeros = (0,) * len(minor)
    old_spec = pl.BlockSpec(blk, lambda s, r: (s * nb + jnp.minimum(r + 1, nb - 1), *zeros))
    new_spec = pl.BlockSpec(blk, lambda s, r: (s, *zeros))
    out_spec = pl.BlockSpec(blk, lambda s, r: (s * nb + r, *zeros))
    block_bytes = 6 * t_dec * math.prod(minor[:-2]) * max(minor[-2], V7X_SUBLANES) * max(minor[-1], V7X_LANES) * 4
    ok, ov = pl.pallas_call(
        _roll_kernel,
        grid=(n_win, nb),
        in_specs=[old_spec, old_spec, new_spec, new_spec],
        out_specs=[out_spec, out_spec],
        out_shape=[jax.ShapeDtypeStruct(flat, caches_k.dtype), jax.ShapeDtypeStruct(flat, caches_v.dtype)],
        compiler_params=_params(("arbitrary", "arbitrary"), block_bytes),
        name="roll_caches",
    )(caches_k.reshape(flat), caches_v.reshape(flat),
      jnp.stack(new_k).reshape(new_shape), jnp.stack(new_v).reshape(new_shape))
    return ok.reshape(shape), ov.reshape(shape)


def _qk_gains(qk_g, n_q_heads, n_k_heads, v_width):
    g = jnp.concatenate([jnp.tile(qk_g[0], n_q_heads), jnp.tile(qk_g[1], n_k_heads), jnp.ones((v_width,), F32)])
    return g.reshape(1, -1).astype(F32)


def kernel(x_prompt, x_sample, cache_a_k, cache_a_v, cache_b_k, cache_b_v, cache_b_logf, cache_c_k, cache_c_v,
           norm_g, ffn_w_in, ffn_w_out, a_w_qkv, a_w_o, a_qk_g, a_rel_bias, b_w_qkvf, b_f_bias, b_w_o, b_qk_g,
           c_w_qkv, c_w_o, c_qk_g, c_sinks, t5_bias):
    batch, seq, d = x_prompt.shape
    n_seq, t_dec, _ = x_sample.shape
    assert batch == 1 and t_dec == CHUNK and seq % 256 == 0
    depth = norm_g.shape[0]
    past_len = cache_b_k.shape[2]
    assert past_len % CHUNK == 0
    n_prompt = batch * seq
    n_samp = n_seq * t_dec
    m = n_prompt + n_samp

    h_a, hd_a = a_rel_bias.shape[2], a_qk_g.shape[2]
    h_b, hd_b = b_f_bias.shape[1], b_qk_g.shape[2]
    h_c, hd_c = c_sinks.shape[1], c_qk_g.shape[2]
    kv_c = cache_c_k.shape[3]
    g_c = h_c // kv_c
    nb_c = cache_c_k.shape[2] // CHUNK
    assert hd_a == V7X_LANES and hd_b == V7X_LANES

    x = jnp.concatenate([x_prompt.reshape(n_prompt, d), x_sample.reshape(n_samp, d)], axis=0)

    ffn_w_out16 = ffn_w_out.astype(BF16)

    def half_ffn(x, g, layer, half, split_rows=False):
        xn = rms_norm_rows(x, g)
        h = ffn_in(xn, ffn_w_in, (layer, half))
        if not split_rows:
            return matmul_residual(h, ffn_w_out16, (layer, half), x, FFN_RES, 512)
        return (matmul_residual(h, ffn_w_out16, (layer, half), x, FFN_RES, 512, 0, n_prompt),
                matmul_residual(h, ffn_w_out16, (layer, half), x, FFN_RES, 512, n_prompt, n_samp))

    a_kp, a_vp, a_ks, a_vs = [], [], [], []
    b_kp, b_vp, b_lp, b_ks, b_vs, b_ls = [], [], [], [], [], []
    c_kp, c_vp, c_ks, c_vs = [], [], [], []

    for i in range(depth):
        kind, j = i % 3, i // 3
        x = half_ffn(x, norm_g[i, 0], i, 0)
        u = rms_norm_rows(x, norm_g[i, 1])
        if kind == 0:
            dq = h_a * hd_a
            gains = _qk_gains(a_qk_g[j], h_a, h_a, dq)
            qkv_p, kv_p = qkv_project(u, 0, n_prompt, a_w_qkv, (j,), gains, hd_a, dq, dq)
            qkv_s, kv_s = qkv_project(u, n_prompt, n_samp, a_w_qkv, (j,), gains, hd_a, dq, dq)
            k_p, v_p, k_s, v_s = kv_p[:, :dq], kv_p[:, dq:], kv_s[:, :dq], kv_s[:, dq:]
            clip = (a_rel_bias.shape[1] - 1) // 2
            o = _band_mixer(qkv_p, qkv_s, m, n_seq, t_dec, cache_a_k, cache_a_v, j,
                            functools.partial(_clip_index, clip), a_rel_bias[j], None,
                            d_q=dq, kv_heads_total=h_a, group=1, hd=hd_a, n_prev=NB_A, past_len=past_len,
                            kv_per_step=8 if h_a % 8 == 0 else 4, tq=256)
            x = matmul_residual(o, a_w_o, (j,), x, 1.0, 1024)
            w = min(NB_A * CHUNK, seq)
            a_kp.append(k_p[n_prompt - w:].reshape(batch, w, h_a, hd_a))
            a_vp.append(v_p[n_prompt - w:].reshape(batch, w, h_a, hd_a))
            a_ks.append(k_s)
            a_vs.append(v_s)
        elif kind == 1:
            dq = h_b * hd_b
            gains = _qk_gains(b_qk_g[j], h_b, h_b, dq)
            qkv_p, kv_p = qkv_project(u, 0, n_prompt, b_w_qkvf, (j,), gains, hd_b, dq, dq)
            qkv_s, kv_s = qkv_project(u, n_prompt, n_samp, b_w_qkvf, (j,), gains, hd_b, dq, dq)
            k_p, v_p, k_s, v_s = kv_p[:, :dq], kv_p[:, dq:], kv_s[:, :dq], kv_s[:, dq:]
            w_f = jnp.pad(b_w_qkvf[j, :, 3 * dq:].astype(BF16), ((0, 0), (0, V7X_LANES - h_b)))
            b_f = jnp.pad(b_f_bias[j].astype(F32), (0, V7X_LANES - h_b)).reshape(1, V7X_LANES)
            logf = forget_gate(u, w_f, b_f)
            tq = 1024 if seq % 1024 == 0 else 256
            cum_p = cumsum_time(logf, batch, seq, 256)
            ck = cum_p[0, :, :h_b].T.reshape(h_b, seq // tq, 1, tq)
            o = fox_prompt(qkv_p, cum_p, ck, jax.ShapeDtypeStruct((m, dq), BF16),
                           heads=h_b, hd=hd_b, seq=seq, tq=tq)
            ls = past_len + t_dec
            logf_all = jnp.concatenate(
                [jnp.pad(cache_b_logf[j].astype(F32), ((0, 0), (0, 0), (0, V7X_LANES - h_b))),
                 logf[n_prompt:].reshape(n_seq, t_dec, V7X_LANES)], axis=1)
            cum_s = cumsum_time(logf_all.reshape(n_seq * ls, V7X_LANES), n_seq, ls, CHUNK)
            ck = cum_s[:, :, :h_b].transpose(0, 2, 1)[:, :, None, :]
            o = fox_sample(qkv_s, cache_b_k[j].reshape(n_seq * past_len, h_b, hd_b),
                           cache_b_v[j].reshape(n_seq * past_len, h_b, hd_b), cum_s, ck, o, n_prompt // t_dec,
                           n_seq=n_seq, heads=h_b, hd=hd_b, t_dec=t_dec, past=past_len)
            x = matmul_residual(o, b_w_o, (j,), x, 1.0, 1024)
            b_kp.append(k_p.reshape(batch, seq, h_b, hd_b))
            b_vp.append(v_p.reshape(batch, seq, h_b, hd_b))
            b_lp.append(logf[:n_prompt, :h_b].reshape(batch, seq, h_b))
            b_ks.append(k_s.reshape(n_seq, t_dec, h_b, hd_b))
            b_vs.append(v_s.reshape(n_seq, t_dec, h_b, hd_b))
            b_ls.append(logf[n_prompt:, :h_b].reshape(n_seq, t_dec, h_b))
        else:
            dq, kvw = h_c * hd_c, kv_c * hd_c
            gains = _qk_gains(c_qk_g[j], h_c, kv_c, kvw)
            qkv_p, kv_p = qkv_project(u, 0, n_prompt, c_w_qkv, (j,), gains, hd_c, dq, kvw)
            qkv_s, kv_s = qkv_project(u, n_prompt, n_samp, c_w_qkv, (j,), gains, hd_c, dq, kvw)
            k_p, v_p, k_s, v_s = kv_p[:, :kvw], kv_p[:, kvw:], kv_s[:, :kvw], kv_s[:, kvw:]
            o = _band_mixer(qkv_p, qkv_s, m, n_seq, t_dec, cache_c_k, cache_c_v, j,
                            functools.partial(_t5_bucket, t5_bias.shape[0]), t5_bias, c_sinks[j].astype(F32),
                            d_q=dq, kv_heads_total=kv_c, group=g_c, hd=hd_c, n_prev=nb_c, past_len=past_len,
                            kv_per_step=2, tq=256)
            x = matmul_residual(o, c_w_o, (j,), x, 1.0, 1024)
            w = min(nb_c * CHUNK, seq)
            c_kp.append(k_p[n_prompt - w:].reshape(batch, w, kv_c, hd_c))
            c_vp.append(v_p[n_prompt - w:].reshape(batch, w, kv_c, hd_c))
            c_ks.append(k_s)
            c_vs.append(v_s)
        x = half_ffn(x, norm_g[i, 2], i, 1, split_rows=(i == depth - 1))

    y_prompt = x[0].reshape(batch, seq, d)
    y_sample = x[1].reshape(n_seq, t_dec, d)
    a_k_sample, a_v_sample = roll_caches(cache_a_k, cache_a_v, a_ks, a_vs, t_dec)
    c_k_sample, c_v_sample = roll_caches(cache_c_k, cache_c_v, c_ks, c_vs, t_dec)
    return (y_prompt, y_sample,
            jnp.stack(a_kp), jnp.stack(a_vp), a_k_sample, a_v_sample,
            jnp.stack(b_kp), jnp.stack(b_vp), jnp.stack(b_lp), jnp.stack(b_ks), jnp.stack(b_vs), jnp.stack(b_ls),
            jnp.stack(c_kp), jnp.stack(c_vp), c_k_sample, c_v_sample)
```
